```python
import jax, jax.numpy as jnp
from jax import lax
import numpy as np

D_MODEL = 2048
BATCH = 1
SEQ = 8192
DEPTH = 4
DEC_BATCH = 8
DEC_SEQ = 64
PAST_LEN = 2048

CHUNK = 64
D_MIX = D_MODEL
D_A = D_MIX // 2
D_B = D_MIX - D_A
HGRN_DK = 128
H_A = D_A // HGRN_DK
HGRN_DV = D_A // H_A
H_B = 4
RET_DK = D_B // H_B
RET_DV = D_B // H_B
D_FF = 4 * D_MODEL
N_PROJ = 4 * D_A + 4 * D_B
N_MOD = 6
EPS = 1e-6
ROPE_BASE = 10000.0

kernel_name = "hymba_hgrn2_retention_stream_step"


def rms_norm(x, g):
    xf = x.astype(jnp.float32)
    y = xf * lax.rsqrt(jnp.mean(xf * xf, axis=-1, keepdims=True) + EPS)
    return y * g.astype(jnp.float32)


def to_chunks(t, c):
    b, l, h, d = t.shape
    return t.reshape(b, l // c, c, h, d).transpose(1, 0, 3, 2, 4)


def from_chunks(t):
    n, b, h, c, d = t.shape
    return t.transpose(1, 0, 3, 2, 4).reshape(b, n * c, h, d)


def rope(x, pos):
    half = x.shape[-1] // 2
    inv_freq = 1.0 / (ROPE_BASE ** jnp.linspace(0.0, 1.0, half, dtype=jnp.float32))
    ang = pos[:, None] * inv_freq[None, :]
    cos = jnp.cos(ang)[None, :, None, :]
    sin = jnp.sin(ang)[None, :, None, :]
    x1, x2 = x[..., :half], x[..., half:]
    return jnp.concatenate([x1 * cos - x2 * sin, x2 * cos + x1 * sin], axis=-1)


def hgrn2_chunkwise(q, k, v, log_f, s0):
    L = q.shape[1]
    c = min(CHUNK, L)
    mask = jnp.tril(jnp.ones((c, c), dtype=bool))
    xs = (to_chunks(q, c), to_chunks(k, c), to_chunks(v, c), to_chunks(log_f, c))

    def step(S, inp):
        qc, kc, vc, gc = inp
        b = jnp.cumsum(gc, axis=2)
        o_inter = jnp.einsum('bhtk,bhkv->bhtv', qc * jnp.exp(b), S)
        diff = b[:, :, :, None, :] - b[:, :, None, :, :]
        decay = jnp.exp(jnp.where(mask[None, None, :, :, None], diff, -jnp.inf))
        A = jnp.einsum('bhtk,bhsk,bhtsk->bhts', qc, kc, decay)
        o = o_inter + jnp.einsum('bhts,bhsv->bhtv', A, vc)
        b_last = b[:, :, -1:, :]
        S_new = jnp.exp(b_last[:, :, 0, :])[..., None] * S + jnp.einsum('bhsk,bhsv->bhkv', kc * jnp.exp(b_last - b), vc)
        return S_new, o

    s_final, o = lax.scan(step, s0.astype(jnp.float32), xs)
    return from_chunks(o), s_final


def retention_chunkwise(q, k, v, log_gamma, s0):
    L = q.shape[1]
    c = min(CHUNK, L)
    idx = jnp.arange(c, dtype=jnp.float32)
    lg = log_gamma[:, None, None]
    dmat = jnp.where(jnp.tril(jnp.ones((c, c), dtype=bool))[None],
                     jnp.exp((idx[:, None] - idx[None, :])[None] * lg), 0.0)
    xi = jnp.exp((idx + 1.0)[None, :] * log_gamma[:, None])[..., None]
    zeta = jnp.exp((c - 1.0 - idx)[None, :] * log_gamma[:, None])[..., None]
    g_c = jnp.exp(c * log_gamma)[:, None, None]
    xs = (to_chunks(q, c), to_chunks(k, c), to_chunks(v, c))

    def step(S, inp):
        qc, kc, vc = inp
        A = jnp.einsum('bhtk,bhsk->bhts', qc, kc) * dmat
        o = jnp.einsum('bhts,bhsv->bhtv', A, vc) + jnp.einsum('bhtk,bhkv->bhtv', qc, S) * xi
        S_new = g_c * S + jnp.einsum('bhsk,bhsv->bhkv', kc * zeta, vc)
        return S_new, o

    s_final, o = lax.scan(step, s0.astype(jnp.float32), xs)
    return from_chunks(o), s_final


def mixer(h, pos, s_a, s_b, lb, w_in, g_hn, g_rn, w_out, log_gamma):
    B, L, _ = h.shape
    proj = (h.astype(w_in.dtype) @ w_in).astype(jnp.float32)
    cuts = [D_A, 2 * D_A, 3 * D_A, 4 * D_A, 4 * D_A + D_B, 4 * D_A + 2 * D_B, 4 * D_A + 3 * D_B]
    qa, fa, ia, ga, qb, kb, vb, gb = jnp.split(proj, cuts, axis=-1)
    q_a = jax.nn.silu(qa.reshape(B, L, H_A, HGRN_DK))
    lb_h = lb.astype(jnp.float32).reshape(H_A, HGRN_DK)
    f = lb_h + (1.0 - lb_h) * jax.nn.sigmoid(fa.reshape(B, L, H_A, HGRN_DK))
    o_a, s_a_new = hgrn2_chunkwise(q_a, 1.0 - f, ia.reshape(B, L, H_A, HGRN_DV), jnp.log(f), s_a)
    o_a = rms_norm(o_a, g_hn.reshape(H_A, HGRN_DV)) * jax.nn.silu(ga.reshape(B, L, H_A, HGRN_DV))
    q_b = rope(qb.reshape(B, L, H_B, RET_DK), pos)
    k_b = rope(kb.reshape(B, L, H_B, RET_DK), pos) * (RET_DK ** -0.5)
    o_b, s_b_new = retention_chunkwise(q_b, k_b, vb.reshape(B, L, H_B, RET_DV), log_gamma, s_b)
    o_b = rms_norm(o_b, g_rn.reshape(H_B, RET_DV)) * jax.nn.silu(gb.reshape(B, L, H_B, RET_DV))
    o = jnp.concatenate([o_a.reshape(B, L, D_A), o_b.reshape(B, L, D_B)], axis=-1)
    return o.astype(w_out.dtype) @ w_out, s_a_new, s_b_new


def trunk(x, c, pos, s_a, s_b, lb_all, log_gamma, w_ada, b_ada, norm1_g, norm2_g, w_in,
          hgrn_norm_g, ret_norm_g, w_out, w_up, w_down, final_g):
    new_a, new_b = [], []
    c_act = jax.nn.silu(c.astype(jnp.float32))
    for l in range(DEPTH):
        mod = c_act @ w_ada[l].astype(jnp.float32) + b_ada[l].astype(jnp.float32)
        sh1, sc1, gt1, sh2, sc2, gt2 = [m[:, None, :] for m in jnp.split(mod, N_MOD, axis=-1)]
        h = rms_norm(x, norm1_g[l]) * (1.0 + sc1) + sh1
        o, sa, sb = mixer(h, pos, s_a[l], s_b[l], lb_all[l], w_in[l], hgrn_norm_g[l], ret_norm_g[l], w_out[l], log_gamma)
        x = (x.astype(jnp.float32) + gt1 * o.astype(jnp.float32)).astype(x.dtype)
        new_a.append(sa)
        new_b.append(sb)
        h = rms_norm(x, norm2_g[l]) * (1.0 + sc2) + sh2
        u = jnp.square(jax.nn.relu(h.astype(w_up.dtype) @ w_up[l]))
        y = u @ w_down[l]
        x = (x.astype(jnp.float32) + gt2 * y.astype(jnp.float32)).astype(x.dtype)
    y_out = rms_norm(x, final_g).astype(x.dtype)
    return y_out, jnp.stack(new_a, axis=0), jnp.stack(new_b, axis=0)


def setup_inputs(seed: int = 0) -> dict:
    key = jax.random.key(seed)
    ks = jax.random.split(key, 20)
    f32 = jnp.float32
    nrm = lambda k, shape, s: jax.random.normal(k, shape, f32) * s
    return {
        "x_prompt": nrm(ks[0], (BATCH, SEQ, D_MODEL), 1.0),
        "x_sample": nrm(ks[1], (DEC_BATCH, DEC_SEQ, D_MODEL), 1.0),
        "state_hgrn": nrm(ks[2], (DEPTH, DEC_BATCH, H_A, HGRN_DK, HGRN_DV), 0.5),
        "state_ret": nrm(ks[3], (DEPTH, DEC_BATCH, H_B, RET_DK, RET_DV), 0.5),
        "c_prompt": nrm(ks[4], (BATCH, D_MODEL), 1.0),
        "c_sample": nrm(ks[5], (DEC_BATCH, D_MODEL), 1.0),
        "lb_logits": nrm(ks[6], (DEPTH, D_A), 0.5),
        "w_ada": nrm(ks[7], (DEPTH, D_MODEL, N_MOD * D_MODEL), 0.5 * D_MODEL ** -0.5),
        "b_ada": nrm(ks[8], (DEPTH, N_MOD * D_MODEL), 0.02),
        "norm1_g": 1.0 + nrm(ks[9], (DEPTH, D_MODEL), 0.02),
        "norm2_g": 1.0 + nrm(ks[10], (DEPTH, D_MODEL), 0.02),
        "w_in": nrm(ks[11], (DEPTH, D_MODEL, N_PROJ), D_MODEL ** -0.5),
        "hgrn_norm_g": 1.0 + nrm(ks[12], (DEPTH, D_A), 0.02),
        "ret_norm_g": 1.0 + nrm(ks[13], (DEPTH, D_B), 0.02),
        "w_out": nrm(ks[14], (DEPTH, D_MIX, D_MODEL), D_MIX ** -0.5),
        "w_up": nrm(ks[15], (DEPTH, D_MODEL, D_FF), D_MODEL ** -0.5),
        "w_down": nrm(ks[16], (DEPTH, D_FF, D_MODEL), D_FF ** -0.5),
        "final_g": 1.0 + nrm(ks[17], (D_MODEL,), 0.02),
    }


def reference(x_prompt, x_sample, state_hgrn, state_ret, c_prompt, c_sample, lb_logits, w_ada, b_ada,
              norm1_g, norm2_g, w_in, hgrn_norm_g, ret_norm_g, w_out, w_up, w_down, final_g):
    p = jax.nn.softmax(lb_logits.astype(jnp.float32), axis=0)
    cs = jnp.cumsum(p, axis=0)
    lb_all = cs - cs[0:1]
    log_gamma = jnp.log1p(-jnp.exp2(-5.0 - jnp.arange(H_B, dtype=jnp.float32)))
    weights = (w_ada, b_ada, norm1_g, norm2_g, w_in, hgrn_norm_g, ret_norm_g, w_out, w_up, w_down, final_g)
    bp, lp = x_prompt.shape[0], x_prompt.shape[1]
    ls = x_sample.shape[1]
    pos_p = jnp.arange(lp, dtype=jnp.float32)
    pos_s = PAST_LEN + jnp.arange(ls, dtype=jnp.float32)
    zeros_a = jnp.zeros((DEPTH, bp, H_A, HGRN_DK, HGRN_DV), jnp.float32)
    zeros_b = jnp.zeros((DEPTH, bp, H_B, RET_DK, RET_DV), jnp.float32)
    y_prompt, sa_p, sb_p = trunk(x_prompt, c_prompt, pos_p, zeros_a, zeros_b, lb_all, log_gamma, *weights)
    y_sample, sa_s, sb_s = trunk(x_sample, c_sample, pos_s, state_hgrn, state_ret, lb_all, log_gamma, *weights)
    return (y_prompt, y_sample, sa_p, sb_p, sa_s, sb_s)
```

```python
import functools

import numpy as np
import jax
import jax.numpy as jnp
from jax import lax
from jax.experimental import pallas as pl
from jax.experimental.pallas import tpu as pltpu

F32 = jnp.float32
BF16 = jnp.bfloat16

D_MODEL = 2048
DEPTH = 4
H_A = 8
H_B = 4
DK_B = 256
N_PROJ = 8192
N_MOD = 6
D_FF = 4 * D_MODEL
EPS = 1e-6
ROPE_BASE = 10000.0
PAST_LEN = 2048

LANES = 128
N_PBLK = N_PROJ // LANES
N_OBLK = D_MODEL // LANES
GROUP = 64
MOD_ROWS = 16

VMEM_LIMIT = 56 * 1024 * 1024

_NT = (((1,), (1,)), ((), ()))
_TN = (((0,), (0,)), ((), ()))


def _cparams(sem):
    return pltpu.CompilerParams(dimension_semantics=sem, vmem_limit_bytes=VMEM_LIMIT)


def _silu(x):
    return x * jax.nn.sigmoid(x)


def _rms(x):
    return x * lax.rsqrt(jnp.mean(x * x, axis=-1, keepdims=True) + EPS)


def _modulate(y, sc, sh, tm):
    y3 = y.reshape(tm // GROUP, GROUP, D_MODEL)
    return (y3 * (1.0 + sc) + sh).reshape(tm, D_MODEL)


def _mod_spec(tm, l, which, ngrid):
    if ngrid == 1:
        return pl.BlockSpec((1, tm // GROUP, 1, D_MODEL), lambda i: (l, i, 0, which))
    return pl.BlockSpec((1, tm // GROUP, 1, D_MODEL), lambda i, j: (l, i, 0, which))


SHIFT1, SCALE1, GATE1, SHIFT2, SCALE2, GATE2 = range(N_MOD)


def _ada_kernel(c_ref, w_ref, b_ref, o_ref):
    c = _silu(c_ref[...]).astype(BF16)
    w = w_ref[0].astype(BF16)
    o_ref[0] = jnp.dot(c, w, preferred_element_type=F32) + b_ref[0]


def _ada_call(c_all, w_ada, b_ada):
    tn = 1024
    n = N_MOD * D_MODEL
    return pl.pallas_call(
        _ada_kernel,
        grid=(DEPTH, n // tn),
        in_specs=[
            pl.BlockSpec((MOD_ROWS, D_MODEL), lambda l, j: (0, 0)),
            pl.BlockSpec((1, D_MODEL, tn), lambda l, j: (l, 0, j)),
            pl.BlockSpec((1, 1, tn), lambda l, j: (l, 0, j)),
        ],
        out_specs=pl.BlockSpec((1, MOD_ROWS, tn), lambda l, j: (l, 0, j)),
        out_shape=jax.ShapeDtypeStruct((DEPTH, MOD_ROWS, n), F32),
        compiler_params=_cparams(("arbitrary", "arbitrary")),
        name="ada_mod",
    )(c_all, w_ada, b_ada.reshape(DEPTH, 1, n))


def _inproj_kernel(x_ref, g_ref, sc_ref, sh_ref, w_ref, o_ref, h_scr, *, tm, tn):
    @pl.when(pl.program_id(1) == 0)
    def _():
        y = _rms(x_ref[...]) * g_ref[0]
        h_scr[...] = _modulate(y, sc_ref[0], sh_ref[0], tm).astype(BF16)

    acc = jnp.dot(h_scr[...], w_ref[0], preferred_element_type=F32)
    for c in range(tn // LANES):
        o_ref[c] = acc[:, c * LANES:(c + 1) * LANES]


def _inproj_call(x, modg, norm1_g, w_in, l, tm=512, tn=2048):
    t = x.shape[0]
    kern = functools.partial(_inproj_kernel, tm=tm, tn=tn)
    return pl.pallas_call(
        kern,
        grid=(t // tm, N_PROJ // tn),
        in_specs=[
            pl.BlockSpec((tm, D_MODEL), lambda i, j: (i, 0)),
            pl.BlockSpec((1, 1, D_MODEL), lambda i, j: (l, 0, 0)),
            _mod_spec(tm, l, SCALE1, 2),
            _mod_spec(tm, l, SHIFT1, 2),
            pl.BlockSpec((1, D_MODEL, tn), lambda i, j: (l, 0, j)),
        ],
        out_specs=pl.BlockSpec((tn // LANES, tm, LANES), lambda i, j: (j, i, 0)),
        out_shape=jax.ShapeDtypeStruct((N_PBLK, t, LANES), F32),
        scratch_shapes=[pltpu.VMEM((tm, D_MODEL), BF16)],
        compiler_params=_cparams(("arbitrary", "arbitrary")),
        name="in_proj",
    )(x, norm1_g, modg, modg, w_in)


def _hgrn_ref_rows(b_scr, lvl, c):
    half = 1 << lvl
    m = 2 * half
    if half >= 8:
        blocks = [jnp.broadcast_to(b_scr[pl.ds(j * m + half - 1, 1), :], (m, LANES))
                  for j in range(c // m)]
    elif half == 4:
        blocks = [jnp.broadcast_to(b_scr[pl.ds(8 * j + 3, 1), :], (8, LANES))
                  for j in range(c // 8)]
    else:
        sub = lax.broadcasted_iota(jnp.int32, (8, LANES), 0)
        blocks = [jnp.where(sub < 4,
                            jnp.broadcast_to(b_scr[pl.ds(8 * j + 1, 1), :], (8, LANES)),
                            jnp.broadcast_to(b_scr[pl.ds(8 * j + 5, 1), :], (8, LANES)))
                  for j in range(c // 8)]
    return blocks[0] if len(blocks) == 1 else jnp.concatenate(blocks, axis=0)


def _mixer_kernel(*refs, c, n_chunks, has_state):
    if has_state:
        (proj_ref, cos_ref, sin_ref, lb_ref, ghn_ref, grn_ref, lv_ref, tri_ref, dmat_ref, xi_ref,
         zeta_ref, gc_ref, sa_in_ref, sb_in_ref, o_ref, sa_out_ref, sb_out_ref,
         st_scr, sb_scr, b_scr) = refs
    else:
        (proj_ref, cos_ref, sin_ref, lb_ref, ghn_ref, grn_ref, lv_ref, tri_ref, dmat_ref, xi_ref,
         zeta_ref, gc_ref, o_ref, sa_out_ref, sb_out_ref, st_scr, sb_scr, b_scr) = refs
    ci = pl.program_id(1)
    n_lvl = c.bit_length() - 1

    @pl.when(ci == 0)
    def _():
        if has_state:
            for h in range(H_A):
                st_scr[h] = sa_in_ref[0, h].T
            sb_scr[...] = sb_in_ref[0]
        else:
            st_scr[...] = jnp.zeros_like(st_scr)
            sb_scr[...] = jnp.zeros_like(sb_scr)

    row = lax.broadcasted_iota(jnp.int32, (c, LANES), 0)

    def hgrn_head(h, carry):
        qa = proj_ref[h]
        fa = proj_ref[H_A + h]
        v = proj_ref[2 * H_A + h]
        ga = proj_ref[3 * H_A + h]
        lb = lb_ref[h]
        q = _silu(qa)
        f = lb + (1.0 - lb) * jax.nn.sigmoid(fa)
        k = 1.0 - f
        logf = jnp.log(f)
        hi = logf.astype(BF16)
        r1 = logf - hi.astype(F32)
        mid = r1.astype(BF16)
        lo = (r1 - mid.astype(F32)).astype(BF16)
        b3 = jnp.dot(tri_ref[...], jnp.concatenate([hi, mid, lo], axis=-1),
                     preferred_element_type=F32)
        b = b3[:, :LANES] + b3[:, LANES:2 * LANES] + b3[:, 2 * LANES:]
        b_scr[...] = b

        odd = (row & 1) != 0
        z = jnp.where(odd, q * f, k).astype(BF16)
        p = lax.dot_general(z, z, _NT, preferred_element_type=F32)
        a = jnp.where(lv_ref[...] == 0, p, 0.0)
        for lvl in range(1, n_lvl):
            e = jnp.exp(-jnp.abs(b - _hgrn_ref_rows(b_scr, lvl, c)))
            z = (jnp.where((row & (1 << lvl)) != 0, q, k) * e).astype(BF16)
            p = lax.dot_general(z, z, _NT, preferred_element_type=F32)
            a = jnp.where(lv_ref[...] == lvl, p, a)

        vb = v.astype(BF16)
        st = st_scr[h]
        o = jnp.dot(a.astype(BF16), vb, preferred_element_type=F32)
        o = o + lax.dot_general((q * jnp.exp(b)).astype(BF16), st.astype(BF16), _NT,
                                preferred_element_type=F32)
        o = o + jnp.sum(q * k, axis=-1, keepdims=True) * v

        b_last = b[c - 1:c, :]
        kd = (k * jnp.exp(b_last - b)).astype(BF16)
        st_scr[h] = st * jnp.exp(b_last) + lax.dot_general(vb, kd, _TN, preferred_element_type=F32)

        o_ref[h] = (_rms(o) * ghn_ref[h] * _silu(ga)).astype(BF16)
        return carry

    lax.fori_loop(0, H_A, hgrn_head, 0)

    cos = cos_ref[...]
    sin = sin_ref[...]

    def ret_head(hb, carry):
        def pair(base):
            return proj_ref[base + 2 * hb], proj_ref[base + 2 * hb + 1]

        q1, q2 = pair(4 * H_A)
        k1, k2 = pair(4 * H_A + 2 * H_B)
        v1, v2 = pair(4 * H_A + 4 * H_B)
        g1, g2 = pair(4 * H_A + 6 * H_B)
        q = jnp.concatenate([q1 * cos - q2 * sin, q2 * cos + q1 * sin], axis=-1)
        k = jnp.concatenate([k1 * cos - k2 * sin, k2 * cos + k1 * sin], axis=-1) * (DK_B ** -0.5)
        qb = q.astype(BF16)
        vb = jnp.concatenate([v1, v2], axis=-1).astype(BF16)
        s = sb_scr[hb]
        a = lax.dot_general(qb, k.astype(BF16), _NT, preferred_element_type=F32) * dmat_ref[hb]
        o = jnp.dot(a.astype(BF16), vb, preferred_element_type=F32)
        o = o + jnp.dot(qb, s.astype(BF16), preferred_element_type=F32) * xi_ref[hb]
        kz = (k * zeta_ref[hb]).astype(BF16)
        sb_scr[hb] = gc_ref[hb] * s + lax.dot_general(kz, vb, _TN, preferred_element_type=F32)
        out = _rms(o) * grn_ref[hb] * _silu(jnp.concatenate([g1, g2], axis=-1))
        o_ref[H_A + 2 * hb] = out[:, :LANES].astype(BF16)
        o_ref[H_A + 2 * hb + 1] = out[:, LANES:].astype(BF16)
        return carry

    lax.fori_loop(0, H_B, ret_head, 0)

    @pl.when(ci == n_chunks - 1)
    def _():
        for h in range(H_A):
            sa_out_ref[0, h] = st_scr[h].T
        sb_out_ref[0] = sb_scr[...]


def _mixer_tables(c, log_gamma):
    t = np.arange(c)
    x = t[:, None] ^ t[None, :]
    lv = np.where(t[:, None] > t[None, :], np.floor(np.log2(np.maximum(x, 1))), -1).astype(np.int32)
    tri = (t[:, None] >= t[None, :]).astype(np.float32)
    idx = jnp.arange(c, dtype=F32)
    lg = log_gamma[:, None, None]
    dmat = jnp.where(jnp.asarray(tri, dtype=bool)[None],
                     jnp.exp((idx[:, None] - idx[None, :])[None] * lg), 0.0)
    xi = jnp.exp((idx + 1.0)[None, :] * log_gamma[:, None])[..., None]
    zeta = jnp.exp((c - 1.0 - idx)[None, :] * log_gamma[:, None])[..., None]
    g_c = jnp.exp(c * log_gamma)[:, None, None]
    return dict(
        lv=jnp.asarray(lv), tri=jnp.asarray(tri, dtype=BF16), dmat=dmat,
        xi=jnp.broadcast_to(xi, (H_B, c, DK_B)), zeta=jnp.broadcast_to(zeta, (H_B, c, DK_B)),
        gc=jnp.broadcast_to(g_c, (H_B, 1, DK_B)))


def _mixer_call(proj, cos, sin, lb, ghn, grn, tab, states, *, c, batch, n_chunks, row0, l):
    has_state = states is not None
    blk0 = row0 // c
    const3 = lambda b, i: (0, 0, 0)
    in_specs = [
        pl.BlockSpec((N_PBLK, c, LANES), lambda b, i: (0, blk0 + b * n_chunks + i, 0)),
        pl.BlockSpec((c, LANES), lambda b, i: (i, 0)),
        pl.BlockSpec((c, LANES), lambda b, i: (i, 0)),
        pl.BlockSpec((H_A, 1, LANES), const3),
        pl.BlockSpec((H_A, 1, LANES), const3),
        pl.BlockSpec((H_B, 1, DK_B), const3),
        pl.BlockSpec((c, c), lambda b, i: (0, 0)),
        pl.BlockSpec((c, c), lambda b, i: (0, 0)),
        pl.BlockSpec((H_B, c, c), const3),
        pl.BlockSpec((H_B, c, DK_B), const3),
        pl.BlockSpec((H_B, c, DK_B), const3),
        pl.BlockSpec((H_B, 1, DK_B), const3),
    ]
    args = [proj, cos, sin, lb, ghn, grn, tab["lv"], tab["tri"], tab["dmat"], tab["xi"],
            tab["zeta"], tab["gc"]]
    if has_state:
        in_specs += [
            pl.BlockSpec((1, H_A, LANES, LANES), lambda b, i: (b, 0, 0, 0)),
            pl.BlockSpec((1, H_B, DK_B, DK_B), lambda b, i: (b, 0, 0, 0)),
        ]
        args += list(states)
    kern = functools.partial(_mixer_kernel, c=c, n_chunks=n_chunks, has_state=has_state)
    return pl.pallas_call(
        kern,
        grid=(batch, n_chunks),
        in_specs=in_specs,
        out_specs=[
            pl.BlockSpec((N_OBLK, c, LANES), lambda b, i: (0, b * n_chunks + i, 0)),
            pl.BlockSpec((1, H_A, LANES, LANES), lambda b, i: (b, 0, 0, 0)),
            pl.BlockSpec((1, H_B, DK_B, DK_B), lambda b, i: (b, 0, 0, 0)),
        ],
        out_shape=[
            jax.ShapeDtypeStruct((N_OBLK, batch * n_chunks * c, LANES), BF16),
            jax.ShapeDtypeStruct((batch, H_A, LANES, LANES), F32),
            jax.ShapeDtypeStruct((batch, H_B, DK_B, DK_B), F32),
        ],
        scratch_shapes=[
            pltpu.VMEM((H_A, LANES, LANES), F32),
            pltpu.VMEM((H_B, DK_B, DK_B), F32),
            pltpu.VMEM((c, LANES), F32),
        ],
        compiler_params=_cparams(("arbitrary", "arbitrary")),
        name="mixer_state" if has_state else "mixer_prompt",
    )(*args)


def _outproj_kernel(op_ref, os_ref, x_ref, w_ref, gt_ref, g2_ref, sc_ref, sh_ref, xo_ref, h_ref,
                    *, tm, n_prompt_tiles):
    i = pl.program_id(0)

    def run(o_ref):
        o = jnp.concatenate([o_ref[cb] for cb in range(N_OBLK)], axis=-1)
        y = jnp.dot(o, w_ref[0], preferred_element_type=F32)
        xn = (x_ref[...].reshape(tm // GROUP, GROUP, D_MODEL)
              + gt_ref[0] * y.reshape(tm // GROUP, GROUP, D_MODEL)).reshape(tm, D_MODEL)
        xo_ref[...] = xn
        h_ref[...] = _modulate(_rms(xn) * g2_ref[0], sc_ref[0], sh_ref[0], tm).astype(BF16)

    @pl.when(i < n_prompt_tiles)
    def _():
        run(op_ref)

    @pl.when(i >= n_prompt_tiles)
    def _():
        run(os_ref)


def _outproj_call(o_p, o_s, x, modg, norm2_g, w_out, l, n_prompt, tm=256):
    t = x.shape[0]
    npt = n_prompt // tm
    kern = functools.partial(_outproj_kernel, tm=tm, n_prompt_tiles=npt)
    last_p = npt - 1
    return pl.pallas_call(
        kern,
        grid=(t // tm,),
        in_specs=[
            pl.BlockSpec((N_OBLK, tm, LANES), lambda i: (0, jnp.minimum(i, last_p), 0)),
            pl.BlockSpec((N_OBLK, tm, LANES), lambda i: (0, jnp.maximum(i - npt, 0), 0)),
            pl.BlockSpec((tm, D_MODEL), lambda i: (i, 0)),
            pl.BlockSpec((1, D_MODEL, D_MODEL), lambda i: (l, 0, 0)),
            _mod_spec(tm, l, GATE1, 1),
            pl.BlockSpec((1, 1, D_MODEL), lambda i: (l, 0, 0)),
            _mod_spec(tm, l, SCALE2, 1),
            _mod_spec(tm, l, SHIFT2, 1),
        ],
        out_specs=[
            pl.BlockSpec((tm, D_MODEL), lambda i: (i, 0)),
            pl.BlockSpec((tm, D_MODEL), lambda i: (i, 0)),
        ],
        out_shape=[
            jax.ShapeDtypeStruct((t, D_MODEL), F32),
            jax.ShapeDtypeStruct((t, D_MODEL), BF16),
        ],
        compiler_params=_cparams(("arbitrary",)),
        name="out_proj",
    )(o_p, o_s, x, w_out, modg, norm2_g, modg, modg)


def _mlp_kernel(h_ref, x_ref, wu_ref, wd_ref, gt_ref, fg_ref, xo_ref, acc_ref, *, tm, nf, final):
    f = pl.program_id(1)
    u = jnp.dot(h_ref[...], wu_ref[0], preferred_element_type=F32)
    u = jnp.square(jnp.maximum(u, 0.0)).astype(BF16)
    y = jnp.dot(u, wd_ref[0], preferred_element_type=F32)

    @pl.when(f == 0)
    def _():
        acc_ref[...] = y

    @pl.when(f > 0)
    def _():
        acc_ref[...] += y

    @pl.when(f == nf - 1)
    def _():
        xn = (x_ref[...].reshape(tm // GROUP, GROUP, D_MODEL)
              + gt_ref[0] * acc_ref[...].reshape(tm // GROUP, GROUP, D_MODEL)
              ).reshape(tm, D_MODEL)
        if final:
            xn = _rms(xn) * fg_ref[...]
        xo_ref[...] = xn


def _mlp_call(h, x, modg, w_up, w_down, final_g, l, final, tm=512, tf=1024):
    t = x.shape[0]
    nf = D_FF // tf
    kern = functools.partial(_mlp_kernel, tm=tm, nf=nf, final=final)
    return pl.pallas_call(
        kern,
        grid=(t // tm, nf),
        in_specs=[
            pl.BlockSpec((tm, D_MODEL), lambda i, f: (i, 0)),
            pl.BlockSpec((tm, D_MODEL), lambda i, f: (i, 0)),
            pl.BlockSpec((1, D_MODEL, tf), lambda i, f: (l, 0, f)),
            pl.BlockSpec((1, tf, D_MODEL), lambda i, f: (l, f, 0)),
            _mod_spec(tm, l, GATE2, 2),
            pl.BlockSpec((1, D_MODEL), lambda i, f: (0, 0)),
        ],
        out_specs=pl.BlockSpec((tm, D_MODEL), lambda i, f: (i, 0)),
        out_shape=jax.ShapeDtypeStruct((t, D_MODEL), F32),
        scratch_shapes=[pltpu.VMEM((tm, D_MODEL), F32)],
        compiler_params=_cparams(("arbitrary", "arbitrary")),
        name="mlp_final" if final else "mlp",
    )(h, x, w_up, w_down, modg, final_g)


def _rope_tables(pos):
    half = DK_B // 2
    inv_freq = 1.0 / (ROPE_BASE ** jnp.linspace(0.0, 1.0, half, dtype=F32))
    ang = pos[:, None] * inv_freq[None, :]
    return jnp.cos(ang), jnp.sin(ang)


def kernel(x_prompt, x_sample, state_hgrn, state_ret, c_prompt, c_sample, lb_logits, w_ada, b_ada,
           norm1_g, norm2_g, w_in, hgrn_norm_g, ret_norm_g, w_out, w_up, w_down, final_g):
    bp, lp, _ = x_prompt.shape
    bs, ls, _ = x_sample.shape
    assert bp == 1 and ls == GROUP and lp % 512 == 0 and (bs * ls) % 512 == 0
    n_prompt = bp * lp
    n_sample = bs * ls
    c_prompt_chunk = 256

    x = jnp.concatenate([x_prompt.reshape(n_prompt, D_MODEL), x_sample.reshape(n_sample, D_MODEL)], 0)

    n_seq = bp + bs
    c_all = jnp.concatenate([c_prompt, c_sample, jnp.zeros((MOD_ROWS - n_seq, D_MODEL), F32)], 0)
    mod = _ada_call(c_all, w_ada, b_ada)
    modg = jnp.concatenate(
        [jnp.broadcast_to(mod[:, 0:1], (DEPTH, lp // GROUP, N_MOD * D_MODEL)), mod[:, bp:n_seq]],
        axis=1)[:, :, None, :]

    p = jax.nn.softmax(lb_logits.astype(F32), axis=0)
    cs = jnp.cumsum(p, axis=0)
    lb_all = (cs - cs[0:1]).reshape(DEPTH, H_A, 1, LANES)
    log_gamma = jnp.log1p(-jnp.exp2(-5.0 - jnp.arange(H_B, dtype=F32)))
    ghn = hgrn_norm_g.reshape(DEPTH, H_A, 1, LANES)
    grn = ret_norm_g.reshape(DEPTH, H_B, 1, DK_B)
    n1 = norm1_g.reshape(DEPTH, 1, D_MODEL)
    n2 = norm2_g.reshape(DEPTH, 1, D_MODEL)
    fg = final_g.reshape(1, D_MODEL)
    w_in_b = w_in.astype(BF16)
    w_out_b = w_out.astype(BF16)
    w_up_b = w_up.astype(BF16)
    w_down_b = w_down.astype(BF16)

    cos_p, sin_p = _rope_tables(jnp.arange(lp, dtype=F32))
    cos_s, sin_s = _rope_tables(PAST_LEN + jnp.arange(ls, dtype=F32))
    tab_p = _mixer_tables(c_prompt_chunk, log_gamma)
    tab_s = _mixer_tables(ls, log_gamma)

    sa_p, sb_p, sa_s, sb_s = [], [], [], []
    for l in range(DEPTH):
        proj = _inproj_call(x, modg, n1, w_in_b, l)
        o_p, sa, sb = _mixer_call(proj, cos_p, sin_p, lb_all[l], ghn[l], grn[l], tab_p, None,
                                  c=c_prompt_chunk, batch=bp, n_chunks=lp // c_prompt_chunk,
                                  row0=0, l=l)
        sa_p.append(sa)
        sb_p.append(sb)
        o_s, sa, sb = _mixer_call(proj, cos_s, sin_s, lb_all[l], ghn[l], grn[l], tab_s,
                                  (state_hgrn[l], state_ret[l]),
                                  c=ls, batch=bs, n_chunks=1, row0=n_prompt, l=l)
        sa_s.append(sa)
        sb_s.append(sb)
        x, h2 = _outproj_call(o_p, o_s, x, modg, n2, w_out_b, l, n_prompt)
        x = _mlp_call(h2, x, modg, w_up_b, w_down_b, fg, l, final=(l == DEPTH - 1))

    y_prompt = x[:n_prompt].reshape(bp, lp, D_MODEL)
    y_sample = x[n_prompt:].reshape(bs, ls, D_MODEL)
    return (y_prompt, y_sample, jnp.stack(sa_p, 0), jnp.stack(sb_p, 0),
            jnp.stack(sa_s, 0), jnp.stack(sb_s, 0))
```

```python
import functools

import numpy as np
import jax
import jax.numpy as jnp
from jax import lax
from jax.experimental import pallas as pl
from jax.experimental.pallas import tpu as pltpu

F32 = jnp.float32
BF16 = jnp.bfloat16

D_MODEL = 2048
DEPTH = 4
H_A = 8
H_B = 4
DK_B = 256
N_PROJ = 8192
N_MOD = 6
D_FF = 4 * D_MODEL
EPS = 1e-6
ROPE_BASE = 10000.0
PAST_LEN = 2048

LANES = 128
N_PBLK = N_PROJ // LANES
N_OBLK = D_MODEL // LANES
GROUP = 64
MOD_ROWS = 16

VMEM_LIMIT = 56 * 1024 * 1024

_NT = (((1,), (1,)), ((), ()))
_TN = (((0,), (0,)), ((), ()))

SHIFT1, SCALE1, GATE1, SHIFT2, SCALE2, GATE2 = range(N_MOD)


def _cparams(sem):
    return pltpu.CompilerParams(dimension_semantics=sem, vmem_limit_bytes=VMEM_LIMIT)


def _silu(x):
    return x * jax.nn.sigmoid(x)


def _rms(x):
    return x * lax.rsqrt(jnp.mean(x * x, axis=-1, keepdims=True) + EPS)


def _modulate(y, sc, sh, tm):
    y3 = y.reshape(tm // GROUP, GROUP, D_MODEL)
    return (y3 * (1.0 + sc) + sh).reshape(tm, D_MODEL)


def _gated_add(x, gt, y, tm):
    return (x.reshape(tm // GROUP, GROUP, D_MODEL)
            + gt * y.reshape(tm // GROUP, GROUP, D_MODEL)).reshape(tm, D_MODEL)


def _mod_spec(tm, l, which, ngrid):
    if ngrid == 1:
        return pl.BlockSpec((1, tm // GROUP, 1, D_MODEL), lambda i: (l, i, 0, which))
    return pl.BlockSpec((1, tm // GROUP, 1, D_MODEL), lambda i, j: (l, i, 0, which))


def _ada_kernel(c_ref, w_ref, b_ref, o_ref):
    c = _silu(c_ref[...]).astype(BF16)
    w = w_ref[0].astype(BF16)
    o_ref[0] = jnp.dot(c, w, preferred_element_type=F32) + b_ref[0]


def _ada_call(c_all, w_ada, b_ada):
    tn = 1024
    n = N_MOD * D_MODEL
    return pl.pallas_call(
        _ada_kernel,
        grid=(DEPTH, n // tn),
        in_specs=[
            pl.BlockSpec((MOD_ROWS, D_MODEL), lambda l, j: (0, 0)),
            pl.BlockSpec((1, D_MODEL, tn), lambda l, j: (l, 0, j)),
            pl.BlockSpec((1, 1, tn), lambda l, j: (l, 0, j)),
        ],
        out_specs=pl.BlockSpec((1, MOD_ROWS, tn), lambda l, j: (l, 0, j)),
        out_shape=jax.ShapeDtypeStruct((DEPTH, MOD_ROWS, n), F32),
        compiler_params=_cparams(("arbitrary", "arbitrary")),
        name="ada_mod",
    )(c_all, w_ada, b_ada.reshape(DEPTH, 1, n))


def _inproj_kernel(x_ref, g_ref, sc_ref, sh_ref, w_ref, o_ref, h_scr, *, tm, tn):
    @pl.when(pl.program_id(1) == 0)
    def _():
        y = _rms(x_ref[...]) * g_ref[0]
        h_scr[...] = _modulate(y, sc_ref[0], sh_ref[0], tm).astype(BF16)

    acc = jnp.dot(h_scr[...], w_ref[...], preferred_element_type=F32)
    for c in range(tn // LANES):
        o_ref[c] = acc[:, c * LANES:(c + 1) * LANES].astype(BF16)


def _inproj_call(x, modg, norm1_g, w_in_l, l, tm=512, tn=2048):
    t = x.shape[0]
    kern = functools.partial(_inproj_kernel, tm=tm, tn=tn)
    return pl.pallas_call(
        kern,
        grid=(t // tm, N_PROJ // tn),
        in_specs=[
            pl.BlockSpec((tm, D_MODEL), lambda i, j: (i, 0)),
            pl.BlockSpec((1, 1, D_MODEL), lambda i, j: (l, 0, 0)),
            _mod_spec(tm, l, SCALE1, 2),
            _mod_spec(tm, l, SHIFT1, 2),
            pl.BlockSpec((D_MODEL, tn), lambda i, j: (0, j)),
        ],
        out_specs=pl.BlockSpec((tn // LANES, tm, LANES), lambda i, j: (j, i, 0)),
        out_shape=jax.ShapeDtypeStruct((N_PBLK, t, LANES), BF16),
        scratch_shapes=[pltpu.VMEM((tm, D_MODEL), BF16)],
        compiler_params=_cparams(("arbitrary", "arbitrary")),
        name="in_proj",
    )(x, norm1_g, modg, modg, w_in_l)


def _hgrn_ref_rows(b_scr, lvl, c):
    half = 1 << lvl
    m = 2 * half
    if half >= 8:
        blocks = [jnp.broadcast_to(b_scr[pl.ds(j * m + half - 1, 1), :], (m, LANES))
                  for j in range(c // m)]
    elif half == 4:
        blocks = [jnp.broadcast_to(b_scr[pl.ds(8 * j + 3, 1), :], (8, LANES))
                  for j in range(c // 8)]
    else:
        sub = lax.broadcasted_iota(jnp.int32, (8, LANES), 0)
        blocks = [jnp.where(sub < 4,
                            jnp.broadcast_to(b_scr[pl.ds(8 * j + 1, 1), :], (8, LANES)),
                            jnp.broadcast_to(b_scr[pl.ds(8 * j + 5, 1), :], (8, LANES)))
                  for j in range(c // 8)]
    return blocks[0] if len(blocks) == 1 else jnp.concatenate(blocks, axis=0)


def _mixer_kernel(*refs, c, ca, n_chunks, has_state, n_cast):
    refs = list(refs)
    (proj_ref, cos_ref, sin_ref, lb_ref, ghn_ref, grn_ref, lv_ref, tri_ref, dmat_ref, xi_ref,
     zeta_ref, gc_ref) = refs[:12]
    pos = 12
    if has_state:
        sa_in_ref, sb_in_ref = refs[pos:pos + 2]
        pos += 2
    cast_in = refs[pos:pos + n_cast]
    pos += n_cast
    o_ref, sa_out_ref, sb_out_ref = refs[pos:pos + 3]
    pos += 3
    cast_out = refs[pos:pos + n_cast]
    pos += n_cast
    st_scr, sb_scr, b_scr = refs[pos:]

    ci = pl.program_id(1)
    n_lvl = ca.bit_length() - 1

    @pl.when(ci == 0)
    def _():
        if has_state:
            for h in range(H_A):
                st_scr[h] = sa_in_ref[0, h].T
            sb_scr[...] = sb_in_ref[0]
        else:
            st_scr[...] = jnp.zeros_like(st_scr)
            sb_scr[...] = jnp.zeros_like(sb_scr)

    for src, dst in zip(cast_in, cast_out):
        dst[...] = src[0].astype(BF16)

    row = lax.broadcasted_iota(jnp.int32, (ca, LANES), 0)

    def hgrn_head(h, carry):
        lb = lb_ref[h]
        gain = ghn_ref[h]
        for sub in range(c // ca):
            rows = pl.ds(sub * ca, ca)
            qa = proj_ref[h, rows, :].astype(F32)
            fa = proj_ref[H_A + h, rows, :].astype(F32)
            vb = proj_ref[2 * H_A + h, rows, :]
            ga = proj_ref[3 * H_A + h, rows, :].astype(F32)
            v = vb.astype(F32)
            q = _silu(qa)
            f = lb + (1.0 - lb) * jax.nn.sigmoid(fa)
            k = 1.0 - f
            logf = jnp.log(f)
            hi = logf.astype(BF16)
            r1 = logf - hi.astype(F32)
            mid = r1.astype(BF16)
            lo = (r1 - mid.astype(F32)).astype(BF16)
            b3 = jnp.dot(tri_ref[...], jnp.concatenate([hi, mid, lo], axis=-1),
                         preferred_element_type=F32)
            b = b3[:, :LANES] + b3[:, LANES:2 * LANES] + b3[:, 2 * LANES:]
            b_sub = b_scr.at[sub]
            b_sub[...] = b

            odd = (row & 1) != 0
            z = jnp.where(odd, q * f, k).astype(BF16)
            p = lax.dot_general(z, z, _NT, preferred_element_type=F32)
            a = jnp.where(lv_ref[...] == 0, p, 0.0)
            for lvl in range(1, n_lvl):
                e = jnp.exp(-jnp.abs(b - _hgrn_ref_rows(b_sub, lvl, ca)))
                z = (jnp.where((row & (1 << lvl)) != 0, q, k) * e).astype(BF16)
                p = lax.dot_general(z, z, _NT, preferred_element_type=F32)
                a = jnp.where(lv_ref[...] == lvl, p, a)

            st = st_scr[h]
            o = jnp.dot(a.astype(BF16), vb, preferred_element_type=F32)
            o = o + lax.dot_general((q * jnp.exp(b)).astype(BF16), st.astype(BF16), _NT,
                                    preferred_element_type=F32)
            o = o + jnp.sum(q * k, axis=-1, keepdims=True) * v

            b_last = b[ca - 1:ca, :]
            kd = (k * jnp.exp(b_last - b)).astype(BF16)
            st_scr[h] = (st * jnp.exp(b_last)
                         + lax.dot_general(vb, kd, _TN, preferred_element_type=F32))

            o_ref[h, rows, :] = (_rms(o) * gain * _silu(ga)).astype(BF16)
        return carry

    lax.fori_loop(0, H_A, hgrn_head, 0, unroll=4)

    cos = cos_ref[...]
    sin = sin_ref[...]

    def ret_head(hb, carry):
        def pair(base):
            return (proj_ref[base + 2 * hb].astype(F32), proj_ref[base + 2 * hb + 1].astype(F32))

        q1, q2 = pair(4 * H_A)
        k1, k2 = pair(4 * H_A + 2 * H_B)
        g1, g2 = pair(4 * H_A + 6 * H_B)
        vbase = 4 * H_A + 4 * H_B
        vb = jnp.concatenate([proj_ref[vbase + 2 * hb], proj_ref[vbase + 2 * hb + 1]], axis=-1)
        q = jnp.concatenate([q1 * cos - q2 * sin, q2 * cos + q1 * sin], axis=-1)
        k = jnp.concatenate([k1 * cos - k2 * sin, k2 * cos + k1 * sin], axis=-1) * (DK_B ** -0.5)
        qb = q.astype(BF16)
        s = sb_scr[hb]
        a = lax.dot_general(qb, k.astype(BF16), _NT, preferred_element_type=F32) * dmat_ref[hb]
        o = jnp.dot(a.astype(BF16), vb, preferred_element_type=F32)
        o = o + jnp.dot(qb, s.astype(BF16), preferred_element_type=F32) * xi_ref[hb]
        kz = (k * zeta_ref[hb]).astype(BF16)
        sb_scr[hb] = gc_ref[hb] * s + lax.dot_general(kz, vb, _TN, preferred_element_type=F32)
        out = _rms(o) * grn_ref[hb] * _silu(jnp.concatenate([g1, g2], axis=-1))
        o_ref[H_A + 2 * hb] = out[:, :LANES].astype(BF16)
        o_ref[H_A + 2 * hb + 1] = out[:, LANES:].astype(BF16)
        return carry

    lax.fori_loop(0, H_B, ret_head, 0, unroll=4)

    @pl.when(ci == n_chunks - 1)
    def _():
        for h in range(H_A):
            sa_out_ref[0, h] = st_scr[h].T
        sb_out_ref[0] = sb_scr[...]


def _mixer_tables(c, ca, log_gamma):
    t = np.arange(ca)
    x = t[:, None] ^ t[None, :]
    lv = np.where(t[:, None] > t[None, :], np.floor(np.log2(np.maximum(x, 1))), -1).astype(np.int32)
    tri = (t[:, None] >= t[None, :]).astype(np.float32)
    idx = jnp.arange(c, dtype=F32)
    lg = log_gamma[:, None, None]
    causal = jnp.asarray(np.tril(np.ones((c, c), dtype=bool)))
    dmat = jnp.where(causal[None], jnp.exp((idx[:, None] - idx[None, :])[None] * lg), 0.0)
    xi = jnp.exp((idx + 1.0)[None, :] * log_gamma[:, None])[..., None]
    zeta = jnp.exp((c - 1.0 - idx)[None, :] * log_gamma[:, None])[..., None]
    g_c = jnp.exp(c * log_gamma)[:, None, None]
    return dict(
        lv=jnp.asarray(lv), tri=jnp.asarray(tri, dtype=BF16), dmat=dmat,
        xi=jnp.broadcast_to(xi, (H_B, c, DK_B)), zeta=jnp.broadcast_to(zeta, (H_B, c, DK_B)),
        gc=jnp.broadcast_to(g_c, (H_B, 1, DK_B)))


def _mixer_call(proj, cos, sin, lb, ghn, grn, tab, states, casts, *, c, ca, batch, n_chunks, row0):
    has_state = states is not None
    blk0 = row0 // c
    const3 = lambda b, i: (0, 0, 0)
    in_specs = [
        pl.BlockSpec((N_PBLK, c, LANES), lambda b, i: (0, blk0 + b * n_chunks + i, 0)),
        pl.BlockSpec((c, LANES), lambda b, i: (i, 0)),
        pl.BlockSpec((c, LANES), lambda b, i: (i, 0)),
        pl.BlockSpec((H_A, 1, LANES), const3),
        pl.BlockSpec((H_A, 1, LANES), const3),
        pl.BlockSpec((H_B, 1, DK_B), const3),
        pl.BlockSpec((ca, ca), lambda b, i: (0, 0)),
        pl.BlockSpec((ca, ca), lambda b, i: (0, 0)),
        pl.BlockSpec((H_B, c, c), const3),
        pl.BlockSpec((H_B, c, DK_B), const3),
        pl.BlockSpec((H_B, c, DK_B), const3),
        pl.BlockSpec((H_B, 1, DK_B), const3),
    ]
    args = [proj, cos, sin, lb, ghn, grn, tab["lv"], tab["tri"], tab["dmat"], tab["xi"],
            tab["zeta"], tab["gc"]]
    if has_state:
        in_specs += [
            pl.BlockSpec((1, H_A, LANES, LANES), lambda b, i: (b, 0, 0, 0)),
            pl.BlockSpec((1, H_B, DK_B, DK_B), lambda b, i: (b, 0, 0, 0)),
        ]
        args += list(states)
    out_specs = [
        pl.BlockSpec((N_OBLK, c, LANES), lambda b, i: (0, b * n_chunks + i, 0)),
        pl.BlockSpec((1, H_A, LANES, LANES), lambda b, i: (b, 0, 0, 0)),
        pl.BlockSpec((1, H_B, DK_B, DK_B), lambda b, i: (b, 0, 0, 0)),
    ]
    out_shape = [
        jax.ShapeDtypeStruct((N_OBLK, batch * n_chunks * c, LANES), BF16),
        jax.ShapeDtypeStruct((batch, H_A, LANES, LANES), F32),
        jax.ShapeDtypeStruct((batch, H_B, DK_B, DK_B), F32),
    ]
    assert not casts or batch == 1
    for w, layer in casts:
        _, r, cols = w.shape
        rb = r // n_chunks
        in_specs.append(pl.BlockSpec((1, rb, cols), lambda b, i, layer=layer: (layer, i, 0)))
        args.append(w)
        out_specs.append(pl.BlockSpec((rb, cols), lambda b, i: (i, 0)))
        out_shape.append(jax.ShapeDtypeStruct((r, cols), BF16))
    kern = functools.partial(_mixer_kernel, c=c, ca=ca, n_chunks=n_chunks, has_state=has_state,
                             n_cast=len(casts))
    return pl.pallas_call(
        kern,
        grid=(batch, n_chunks),
        in_specs=in_specs,
        out_specs=out_specs,
        out_shape=out_shape,
        scratch_shapes=[
            pltpu.VMEM((H_A, LANES, LANES), F32),
            pltpu.VMEM((H_B, DK_B, DK_B), F32),
            pltpu.VMEM((c // ca, ca, LANES), F32),
        ],
        compiler_params=_cparams(("arbitrary", "arbitrary")),
        name="mixer_state" if has_state else "mixer_prompt",
    )(*args)


def _outproj_kernel(op_ref, os_ref, x_ref, w_ref, gt_ref, g2_ref, sc_ref, sh_ref, xo_ref, h_ref,
                    *, tm, n_prompt_tiles):
    i = pl.program_id(0)

    def run(o_ref):
        o = jnp.concatenate([o_ref[cb] for cb in range(N_OBLK)], axis=-1)
        y = jnp.dot(o, w_ref[...], preferred_element_type=F32)
        xn = _gated_add(x_ref[...], gt_ref[0], y, tm)
        xo_ref[...] = xn
        h_ref[...] = _modulate(_rms(xn) * g2_ref[0], sc_ref[0], sh_ref[0], tm).astype(BF16)

    @pl.when(i < n_prompt_tiles)
    def _():
        run(op_ref)

    @pl.when(i >= n_prompt_tiles)
    def _():
        run(os_ref)


def _outproj_call(o_p, o_s, x, modg, norm2_g, w_out_l, l, n_prompt, tm=256):
    t = x.shape[0]
    npt = n_prompt // tm
    kern = functools.partial(_outproj_kernel, tm=tm, n_prompt_tiles=npt)
    last_p = npt - 1
    return pl.pallas_call(
        kern,
        grid=(t // tm,),
        in_specs=[
            pl.BlockSpec((N_OBLK, tm, LANES), lambda i: (0, jnp.minimum(i, last_p), 0)),
            pl.BlockSpec((N_OBLK, tm, LANES), lambda i: (0, jnp.maximum(i - npt, 0), 0)),
            pl.BlockSpec((tm, D_MODEL), lambda i: (i, 0)),
            pl.BlockSpec((D_MODEL, D_MODEL), lambda i: (0, 0)),
            _mod_spec(tm, l, GATE1, 1),
            pl.BlockSpec((1, 1, D_MODEL), lambda i: (l, 0, 0)),
            _mod_spec(tm, l, SCALE2, 1),
            _mod_spec(tm, l, SHIFT2, 1),
        ],
        out_specs=[
            pl.BlockSpec((tm, D_MODEL), lambda i: (i, 0)),
            pl.BlockSpec((tm, D_MODEL), lambda i: (i, 0)),
        ],
        out_shape=[
            jax.ShapeDtypeStruct((t, D_MODEL), F32),
            jax.ShapeDtypeStruct((t, D_MODEL), BF16),
        ],
        compiler_params=_cparams(("arbitrary",)),
        name="out_proj",
    )(o_p, o_s, x, w_out_l, modg, norm2_g, modg, modg)


def _mlp_kernel(h_ref, x_ref, wu_ref, wd_ref, gt_ref, fg_ref, xo_ref, acc_ref, *, tm, nf, final):
    f = pl.program_id(1)

    @pl.when(f == 0)
    def _():
        acc_ref[...] = jnp.zeros_like(acc_ref)

    u = jnp.dot(h_ref[...], wu_ref[...], preferred_element_type=F32)
    u = jnp.square(jnp.maximum(u, 0.0)).astype(BF16)
    acc_ref[...] += jnp.dot(u, wd_ref[...], preferred_element_type=F32)

    @pl.when(f == nf - 1)
    def _():
        xn = _gated_add(x_ref[...], gt_ref[0], acc_ref[...], tm)
        if final:
            xn = _rms(xn) * fg_ref[...]
        xo_ref[...] = xn


def _mlp_call(h, x, modg, w_up_l, w_down_l, final_g, l, final, tm=512, tf=1024):
    t = x.shape[0]
    nf = D_FF // tf
    kern = functools.partial(_mlp_kernel, tm=tm, nf=nf, final=final)
    return pl.pallas_call(
        kern,
        grid=(t // tm, nf),
        in_specs=[
            pl.BlockSpec((tm, D_MODEL), lambda i, f: (i, 0)),
            pl.BlockSpec((tm, D_MODEL), lambda i, f: (i, 0)),
            pl.BlockSpec((D_MODEL, tf), lambda i, f: (0, f)),
            pl.BlockSpec((tf, D_MODEL), lambda i, f: (f, 0)),
            _mod_spec(tm, l, GATE2, 2),
            pl.BlockSpec((1, D_MODEL), lambda i, f: (0, 0)),
        ],
        out_specs=pl.BlockSpec((tm, D_MODEL), lambda i, f: (i, 0)),
        out_shape=jax.ShapeDtypeStruct((t, D_MODEL), F32),
        scratch_shapes=[pltpu.VMEM((tm, D_MODEL), F32)],
        compiler_params=_cparams(("arbitrary", "arbitrary")),
        name="mlp_final" if final else "mlp",
    )(h, x, w_up_l, w_down_l, modg, final_g)


def _rope_tables(pos):
    half = DK_B // 2
    inv_freq = 1.0 / (ROPE_BASE ** jnp.linspace(0.0, 1.0, half, dtype=F32))
    ang = pos[:, None] * inv_freq[None, :]
    return jnp.cos(ang), jnp.sin(ang)


def kernel(x_prompt, x_sample, state_hgrn, state_ret, c_prompt, c_sample, lb_logits, w_ada, b_ada,
           norm1_g, norm2_g, w_in, hgrn_norm_g, ret_norm_g, w_out, w_up, w_down, final_g):
    bp, lp, _ = x_prompt.shape
    bs, ls, _ = x_sample.shape
    assert bp == 1 and ls == GROUP and lp % 512 == 0 and (bs * ls) % 512 == 0
    n_prompt = bp * lp
    n_sample = bs * ls
    c_prompt_chunk = 256
    ca_prompt = 128

    x = jnp.concatenate([x_prompt.reshape(n_prompt, D_MODEL), x_sample.reshape(n_sample, D_MODEL)], 0)

    n_seq = bp + bs
    c_all = jnp.concatenate([c_prompt, c_sample, jnp.zeros((MOD_ROWS - n_seq, D_MODEL), F32)], 0)
    mod = _ada_call(c_all, w_ada, b_ada)
    modg = jnp.concatenate(
        [jnp.broadcast_to(mod[:, 0:1], (DEPTH, lp // GROUP, N_MOD * D_MODEL)), mod[:, bp:n_seq]],
        axis=1)[:, :, None, :]

    p = jax.nn.softmax(lb_logits.astype(F32), axis=0)
    cs = jnp.cumsum(p, axis=0)
    lb_all = (cs - cs[0:1]).reshape(DEPTH, H_A, 1, LANES)
    log_gamma = jnp.log1p(-jnp.exp2(-5.0 - jnp.arange(H_B, dtype=F32)))
    ghn = hgrn_norm_g.reshape(DEPTH, H_A, 1, LANES)
    grn = ret_norm_g.reshape(DEPTH, H_B, 1, DK_B)
    n1 = norm1_g.reshape(DEPTH, 1, D_MODEL)
    n2 = norm2_g.reshape(DEPTH, 1, D_MODEL)
    fg = final_g.reshape(1, D_MODEL)

    cos_p, sin_p = _rope_tables(jnp.arange(lp, dtype=F32))
    cos_s, sin_s = _rope_tables(PAST_LEN + jnp.arange(ls, dtype=F32))
    tab_p = _mixer_tables(c_prompt_chunk, ca_prompt, log_gamma)
    tab_s = _mixer_tables(ls, ls, log_gamma)

    w_in_l = w_in[0].astype(BF16)
    sa_p, sb_p, sa_s, sb_s = [], [], [], []
    for l in range(DEPTH):
        proj = _inproj_call(x, modg, n1, w_in_l, l)
        casts = [(w_out, l), (w_up, l), (w_down, l)] + ([(w_in, l + 1)] if l + 1 < DEPTH else [])
        res = _mixer_call(proj, cos_p, sin_p, lb_all[l], ghn[l], grn[l], tab_p, None, casts,
                          c=c_prompt_chunk, ca=ca_prompt, batch=bp,
                          n_chunks=lp // c_prompt_chunk, row0=0)
        o_p, sa, sb, w_out_l, w_up_l, w_down_l = res[:6]
        if l + 1 < DEPTH:
            w_in_l = res[6]
        sa_p.append(sa)
        sb_p.append(sb)
        o_s, sa, sb = _mixer_call(proj, cos_s, sin_s, lb_all[l], ghn[l], grn[l], tab_s,
                                  (state_hgrn[l], state_ret[l]), [],
                                  c=ls, ca=ls, batch=bs, n_chunks=1, row0=n_prompt)
        sa_s.append(sa)
        sb_s.append(sb)
        x, h2 = _outproj_call(o_p, o_s, x, modg, n2, w_out_l, l, n_prompt)
        x = _mlp_call(h2, x, modg, w_up_l, w_down_l, fg, l, final=(l == DEPTH - 1))

    y_prompt = x[:n_prompt].reshape(bp, lp, D_MODEL)
    y_sample = x[n_prompt:].reshape(bs, ls, D_MODEL)
    return (y_prompt, y_sample, jnp.stack(sa_p, 0), jnp.stack(sb_p, 0),
            jnp.stack(sa_s, 0), jnp.stack(sb_s, 0))
```

```python
import functools

import numpy as np
import jax
import jax.numpy as jnp
from jax import lax
from jax.experimental import pallas as pl
from jax.experimental.pallas import tpu as pltpu

F32 = jnp.float32
BF16 = jnp.bfloat16

D_MODEL = 2048
DEPTH = 4
H_A = 8
H_B = 4
DK_B = 256
N_PROJ = 8192
N_MOD = 6
D_FF = 4 * D_MODEL
EPS = 1e-6
ROPE_BASE = 10000.0
LOG2_E = 1.4426950408889634
PAST_LEN = 2048

LANES = 128
N_PBLK = N_PROJ // LANES
N_OBLK = D_MODEL // LANES
GROUP = 64
TM = 512
N_SEQ_S = TM // GROUP
MOD_ROWS = 16

VMEM_LIMIT = 56 * 1024 * 1024

_NT = (((1,), (1,)), ((), ()))
_TN = (((0,), (0,)), ((), ()))

SHIFT1, SCALE1, GATE1, SHIFT2, SCALE2, GATE2 = range(N_MOD)


def _cparams(sem):
    return pltpu.CompilerParams(dimension_semantics=sem, vmem_limit_bytes=VMEM_LIMIT)


def _silu(x):
    return x * jax.nn.sigmoid(x)


def _rms(x):
    return x * lax.rsqrt(jnp.mean(x * x, axis=-1, keepdims=True) + EPS)


def _mod_specs(l, which, ngrid):
    if ngrid == 1:
        return [pl.BlockSpec((1, N_SEQ_S, 1, D_MODEL), lambda i: (l, 0, 0, which)),
                pl.BlockSpec((1, 1, 1, D_MODEL), lambda i: (l, N_SEQ_S, 0, which))]
    return [pl.BlockSpec((1, N_SEQ_S, 1, D_MODEL), lambda i, j: (l, 0, 0, which)),
            pl.BlockSpec((1, 1, 1, D_MODEL), lambda i, j: (l, N_SEQ_S, 0, which))]


def _mod_row(is_sample, s_ref, p_ref, gi):
    if isinstance(is_sample, bool):
        return s_ref[0, gi] if is_sample else p_ref[0, 0]
    return jnp.where(is_sample, s_ref[0, gi], p_ref[0, 0])


def _ada_kernel(c_ref, w_ref, b_ref, o_ref):
    c = _silu(c_ref[...]).astype(BF16)
    w = w_ref[0].astype(BF16)
    o_ref[0] = jnp.dot(c, w, preferred_element_type=F32) + b_ref[0]


def _ada_call(c_all, w_ada, b_ada):
    tn = 1024
    n = N_MOD * D_MODEL
    return pl.pallas_call(
        _ada_kernel,
        grid=(DEPTH, n // tn),
        in_specs=[
            pl.BlockSpec((MOD_ROWS, D_MODEL), lambda l, j: (0, 0)),
            pl.BlockSpec((1, D_MODEL, tn), lambda l, j: (l, 0, j)),
            pl.BlockSpec((1, 1, tn), lambda l, j: (l, 0, j)),
        ],
        out_specs=pl.BlockSpec((1, MOD_ROWS, tn), lambda l, j: (l, 0, j)),
        out_shape=jax.ShapeDtypeStruct((DEPTH, MOD_ROWS, n), F32),
        compiler_params=_cparams(("arbitrary", "arbitrary")),
        name="ada_mod",
    )(c_all, w_ada, b_ada.reshape(DEPTH, 1, n))


def _inproj_kernel(x_ref, g_ref, scs_ref, scp_ref, shs_ref, shp_ref, w_ref, o_ref, r_scr,
                   *, tn, n_prompt_tiles):
    is_sample = pl.program_id(0) >= n_prompt_tiles

    @pl.when(pl.program_id(1) == 0)
    def _():
        x = x_ref[...]
        r_scr[...] = lax.rsqrt(jnp.mean(x * x, axis=-1, keepdims=True) + EPS)

    g = g_ref[0]
    hs = []
    for gi in range(TM // GROUP):
        rows = pl.ds(gi * GROUP, GROUP)
        gain = g * (1.0 + _mod_row(is_sample, scs_ref, scp_ref, gi))
        shift = _mod_row(is_sample, shs_ref, shp_ref, gi)
        hs.append((x_ref[rows, :] * r_scr[rows, :] * gain + shift).astype(BF16))
    h = jnp.concatenate(hs, axis=0)
    acc = jnp.dot(h, w_ref[...], preferred_element_type=F32)
    for c in range(tn // LANES):
        o_ref[c] = acc[:, c * LANES:(c + 1) * LANES].astype(BF16)


def _inproj_call(x, mod, norm1_g, w_in_l, l, n_prompt, tn=2048):
    t = x.shape[0]
    kern = functools.partial(_inproj_kernel, tn=tn, n_prompt_tiles=n_prompt // TM)
    return pl.pallas_call(
        kern,
        grid=(t // TM, N_PROJ // tn),
        in_specs=[
            pl.BlockSpec((TM, D_MODEL), lambda i, j: (i, 0)),
            pl.BlockSpec((1, 1, D_MODEL), lambda i, j: (l, 0, 0)),
            *_mod_specs(l, SCALE1, 2),
            *_mod_specs(l, SHIFT1, 2),
            pl.BlockSpec((D_MODEL, tn), lambda i, j: (0, j)),
        ],
        out_specs=pl.BlockSpec((tn // LANES, TM, LANES), lambda i, j: (j, i, 0)),
        out_shape=jax.ShapeDtypeStruct((N_PBLK, t, LANES), BF16),
        scratch_shapes=[pltpu.VMEM((TM, 1), F32)],
        compiler_params=_cparams(("arbitrary", "arbitrary")),
        name="in_proj",
    )(x, norm1_g, mod, mod, mod, mod, w_in_l)


def _hgrn_ref_rows(b_scr, lvl, c):
    half = 1 << lvl
    m = 2 * half
    if half >= 8:
        blocks = [jnp.broadcast_to(b_scr[pl.ds(j * m + half - 1, 1), :], (m, LANES))
                  for j in range(c // m)]
    elif half == 4:
        blocks = [jnp.broadcast_to(b_scr[pl.ds(8 * j + 3, 1), :], (8, LANES))
                  for j in range(c // 8)]
    else:
        sub = lax.broadcasted_iota(jnp.int32, (8, LANES), 0)
        blocks = [jnp.where(sub < 4,
                            jnp.broadcast_to(b_scr[pl.ds(8 * j + 1, 1), :], (8, LANES)),
                            jnp.broadcast_to(b_scr[pl.ds(8 * j + 5, 1), :], (8, LANES)))
                  for j in range(c // 8)]
    return blocks[0] if len(blocks) == 1 else jnp.concatenate(blocks, axis=0)


def _pick_rows(row, q, k, lvl, c):
    half = 1 << lvl
    if half < 8:
        return jnp.where((row & half) != 0, q, k)
    return jnp.concatenate([(q if j & 1 else k)[j * half:(j + 1) * half]
                            for j in range(c // half)], axis=0)


def _mixer_kernel(*refs, c, ca, n_chunks, has_state, n_alias, n_cast):
    refs = list(refs)
    (proj_ref, cos_ref, sin_ref, lb_ref, ghn_ref, grn_ref, lv_ref, tri_ref, dmat_ref, xi_ref,
     zeta_ref, gc_ref) = refs[:12]
    pos = 12
    if has_state:
        sa_in_ref, sb_in_ref = refs[pos:pos + 2]
        pos += 2
    pos += n_alias
    cast_in = refs[pos:pos + n_cast]
    pos += n_cast
    o_ref, sa_out_ref, sb_out_ref = refs[pos:pos + 3]
    pos += 3
    cast_out = refs[pos:pos + n_cast]
    pos += n_cast
    st_scr, sb_scr, b_scr = refs[pos:]

    ci = pl.program_id(1)
    n_lvl = ca.bit_length() - 1

    @pl.when(ci == 0)
    def _():
        if has_state:
            for h in range(H_A):
                st_scr[h] = sa_in_ref[0, 0, h].T
            sb_scr[...] = sb_in_ref[0, 0]
        else:
            st_scr[...] = jnp.zeros_like(st_scr)
            sb_scr[...] = jnp.zeros_like(sb_scr)

    for src, dst in zip(cast_in, cast_out):
        dst[...] = src[0].astype(BF16)

    row = lax.broadcasted_iota(jnp.int32, (ca, LANES), 0)

    def hgrn_head(h, carry):
        lb = lb_ref[h]
        gain = ghn_ref[h]
        for sub in range(c // ca):
            rows = pl.ds(sub * ca, ca)
            qa = proj_ref[h, rows, :].astype(F32)
            fa = proj_ref[H_A + h, rows, :].astype(F32)
            vb = proj_ref[2 * H_A + h, rows, :]
            ga = proj_ref[3 * H_A + h, rows, :].astype(F32)
            v = vb.astype(F32)
            q = _silu(qa)
            f = lb + (1.0 - lb) * jax.nn.sigmoid(fa)
            k = 1.0 - f
            logf = jnp.log(f) * LOG2_E
            hi = logf.astype(BF16)
            r1 = logf - hi.astype(F32)
            mid = r1.astype(BF16)
            lo = (r1 - mid.astype(F32)).astype(BF16)
            b3 = jnp.dot(tri_ref[...], jnp.concatenate([hi, mid, lo], axis=-1),
                         preferred_element_type=F32)
            b = b3[:, :LANES] + b3[:, LANES:2 * LANES] + b3[:, 2 * LANES:]
            b_sub = b_scr.at[sub]
            b_sub[...] = b

            odd = (row & 1) != 0
            z = jnp.where(odd, q * f, k).astype(BF16)
            p = lax.dot_general(z, z, _NT, preferred_element_type=F32)
            a = jnp.where(lv_ref[...] == 0, p, 0.0)
            for lvl in range(1, n_lvl):
                e = jnp.exp2(-jnp.abs(b - _hgrn_ref_rows(b_sub, lvl, ca)))
                z = (_pick_rows(row, q, k, lvl, ca) * e).astype(BF16)
                p = lax.dot_general(z, z, _NT, preferred_element_type=F32)
                a = jnp.where(lv_ref[...] == lvl, p, a)

            st = st_scr[h]
            o = jnp.dot(a.astype(BF16), vb, preferred_element_type=F32)
            o = o + lax.dot_general((q * jnp.exp2(b)).astype(BF16), st.astype(BF16), _NT,
                                    preferred_element_type=F32)
            o = o + jnp.sum(q * k, axis=-1, keepdims=True) * v

            b_last = b[ca - 1:ca, :]
            kd = (k * jnp.exp2(b_last - b)).astype(BF16)
            st_scr[h] = (st * jnp.exp2(b_last)
                         + lax.dot_general(vb, kd, _TN, preferred_element_type=F32))

            o_ref[h, rows, :] = (_rms(o) * gain * _silu(ga)).astype(BF16)
        return carry

    for h in range(H_A):
        hgrn_head(h, 0)

    cos = cos_ref[...]
    sin = sin_ref[...]

    def ret_head(hb, carry):
        def pair(base):
            return (proj_ref[base + 2 * hb].astype(F32), proj_ref[base + 2 * hb + 1].astype(F32))

        q1, q2 = pair(4 * H_A)
        k1, k2 = pair(4 * H_A + 2 * H_B)
        g1, g2 = pair(4 * H_A + 6 * H_B)
        vbase = 4 * H_A + 4 * H_B
        vb = jnp.concatenate([proj_ref[vbase + 2 * hb], proj_ref[vbase + 2 * hb + 1]], axis=-1)
        q = jnp.concatenate([q1 * cos - q2 * sin, q2 * cos + q1 * sin], axis=-1)
        k = jnp.concatenate([k1 * cos - k2 * sin, k2 * cos + k1 * sin], axis=-1) * (DK_B ** -0.5)
        qb = q.astype(BF16)
        s = sb_scr[hb]
        a = lax.dot_general(qb, k.astype(BF16), _NT, preferred_element_type=F32) * dmat_ref[hb]
        o = jnp.dot(a.astype(BF16), vb, preferred_element_type=F32)
        o = o + jnp.dot(qb, s.astype(BF16), preferred_element_type=F32) * xi_ref[hb]
        kz = (k * zeta_ref[hb]).astype(BF16)
        sb_scr[hb] = gc_ref[hb] * s + lax.dot_general(kz, vb, _TN, preferred_element_type=F32)
        out = _rms(o) * grn_ref[hb] * _silu(jnp.concatenate([g1, g2], axis=-1))
        o_ref[H_A + 2 * hb] = out[:, :LANES].astype(BF16)
        o_ref[H_A + 2 * hb + 1] = out[:, LANES:].astype(BF16)
        return carry

    for hb in range(H_B):
        ret_head(hb, 0)

    @pl.when(ci == n_chunks - 1)
    def _():
        for h in range(H_A):
            sa_out_ref[0, 0, h] = st_scr[h].T
        sb_out_ref[0, 0] = sb_scr[...]


def _mixer_tables(c, ca, log_gamma):
    t = np.arange(ca)
    x = t[:, None] ^ t[None, :]
    lv = np.where(t[:, None] > t[None, :], np.floor(np.log2(np.maximum(x, 1))), -1).astype(np.int32)
    tri = (t[:, None] >= t[None, :]).astype(np.float32)
    idx = jnp.arange(c, dtype=F32)
    lg = log_gamma[:, None, None]
    causal = jnp.asarray(np.tril(np.ones((c, c), dtype=bool)))
    dmat = jnp.where(causal[None], jnp.exp((idx[:, None] - idx[None, :])[None] * lg), 0.0)
    xi = jnp.exp((idx + 1.0)[None, :] * log_gamma[:, None])[..., None]
    zeta = jnp.exp((c - 1.0 - idx)[None, :] * log_gamma[:, None])[..., None]
    g_c = jnp.exp(c * log_gamma)[:, None, None]
    return dict(
        lv=jnp.asarray(lv), tri=jnp.asarray(tri, dtype=BF16), dmat=dmat,
        xi=jnp.broadcast_to(xi, (H_B, c, DK_B)), zeta=jnp.broadcast_to(zeta, (H_B, c, DK_B)),
        gc=jnp.broadcast_to(g_c, (H_B, 1, DK_B)))


def _mixer_call(proj, cos, sin, lb, ghn, grn, tab, states, prev_states, casts, *, c, ca, batch,
                n_chunks, row0, l):
    has_state = states is not None
    blk0 = row0 // c
    const3 = lambda b, i: (0, 0, 0)
    sa_spec = pl.BlockSpec((1, 1, H_A, LANES, LANES), lambda b, i: (l, b, 0, 0, 0))
    sb_spec = pl.BlockSpec((1, 1, H_B, DK_B, DK_B), lambda b, i: (l, b, 0, 0, 0))
    in_specs = [
        pl.BlockSpec((N_PBLK, c, LANES), lambda b, i: (0, blk0 + b * n_chunks + i, 0)),
        pl.BlockSpec((c, LANES), lambda b, i: (i, 0)),
        pl.BlockSpec((c, LANES), lambda b, i: (i, 0)),
        pl.BlockSpec((H_A, 1, LANES), const3),
        pl.BlockSpec((H_A, 1, LANES), const3),
        pl.BlockSpec((H_B, 1, DK_B), const3),
        pl.BlockSpec((ca, ca), lambda b, i: (0, 0)),
        pl.BlockSpec((ca, ca), lambda b, i: (0, 0)),
        pl.BlockSpec((H_B, c, c), const3),
        pl.BlockSpec((H_B, c, DK_B), const3),
        pl.BlockSpec((H_B, c, DK_B), const3),
        pl.BlockSpec((H_B, 1, DK_B), const3),
    ]
    args = [proj, cos, sin, lb, ghn, grn, tab["lv"], tab["tri"], tab["dmat"], tab["xi"],
            tab["zeta"], tab["gc"]]
    if has_state:
        in_specs += [sa_spec, sb_spec]
        args += list(states)
    aliases = {}
    if prev_states is not None:
        for k, arr in enumerate(prev_states):
            aliases[len(args)] = 1 + k
            in_specs.append(pl.BlockSpec(memory_space=pl.ANY))
            args.append(arr)
    out_specs = [
        pl.BlockSpec((N_OBLK, c, LANES), lambda b, i: (0, b * n_chunks + i, 0)), sa_spec, sb_spec]
    out_shape = [
        jax.ShapeDtypeStruct((N_OBLK, batch * n_chunks * c, LANES), BF16),
        jax.ShapeDtypeStruct((DEPTH, batch, H_A, LANES, LANES), F32),
        jax.ShapeDtypeStruct((DEPTH, batch, H_B, DK_B, DK_B), F32),
    ]
    assert not casts or batch == 1
    for w, layer in casts:
        _, r, cols = w.shape
        rb = r // n_chunks
        in_specs.append(pl.BlockSpec((1, rb, cols), lambda b, i, layer=layer: (layer, i, 0)))
        args.append(w)
        out_specs.append(pl.BlockSpec((rb, cols), lambda b, i: (i, 0)))
        out_shape.append(jax.ShapeDtypeStruct((r, cols), BF16))
    kern = functools.partial(_mixer_kernel, c=c, ca=ca, n_chunks=n_chunks, has_state=has_state,
                             n_alias=len(aliases), n_cast=len(casts))
    return pl.pallas_call(
        kern,
        grid=(batch, n_chunks),
        in_specs=in_specs,
        out_specs=out_specs,
        out_shape=out_shape,
        input_output_aliases=aliases,
        scratch_shapes=[
            pltpu.VMEM((H_A, LANES, LANES), F32),
            pltpu.VMEM((H_B, DK_B, DK_B), F32),
            pltpu.VMEM((c // ca, ca, LANES), F32),
        ],
        compiler_params=_cparams(("arbitrary", "arbitrary")),
        name="mixer_state" if has_state else "mixer_prompt",
    )(*args)


def _outproj_kernel(op_ref, os_ref, x_ref, w_ref, gts_ref, gtp_ref, g2_ref, scs_ref, scp_ref,
                    shs_ref, shp_ref, xo_ref, h_ref, *, n_prompt_tiles):
    i = pl.program_id(0)
    half = TM // 2

    def run(o_ref, is_sample):
        for hf in range(2):
            o = jnp.concatenate([o_ref[cb, pl.ds(hf * half, half), :] for cb in range(N_OBLK)],
                                axis=-1)
            y = jnp.dot(o, w_ref[...], preferred_element_type=F32)
            for gl in range(half // GROUP):
                gi = hf * (half // GROUP) + gl
                rows = pl.ds(gi * GROUP, GROUP)
                xn = (x_ref[rows, :] + _mod_row(is_sample, gts_ref, gtp_ref, gi)
                      * y[gl * GROUP:(gl + 1) * GROUP, :])
                xo_ref[rows, :] = xn
                gain = g2_ref[0] * (1.0 + _mod_row(is_sample, scs_ref, scp_ref, gi))
                h_ref[rows, :] = (_rms(xn) * gain
                                  + _mod_row(is_sample, shs_ref, shp_ref, gi)).astype(BF16)

    @pl.when(i < n_prompt_tiles)
    def _():
        run(op_ref, False)

    @pl.when(i >= n_prompt_tiles)
    def _():
        run(os_ref, True)


def _outproj_call(o_p, o_s, x, mod, norm2_g, w_out_l, l, n_prompt):
    t = x.shape[0]
    npt = n_prompt // TM
    kern = functools.partial(_outproj_kernel, n_prompt_tiles=npt)
    last_p = npt - 1
    return pl.pallas_call(
        kern,
        grid=(t // TM,),
        in_specs=[
            pl.BlockSpec((N_OBLK, TM, LANES), lambda i: (0, jnp.minimum(i, last_p), 0)),
            pl.BlockSpec((N_OBLK, TM, LANES), lambda i: (0, jnp.maximum(i - npt, 0), 0)),
            pl.BlockSpec((TM, D_MODEL), lambda i: (i, 0)),
            pl.BlockSpec((D_MODEL, D_MODEL), lambda i: (0, 0)),
            *_mod_specs(l, GATE1, 1),
            pl.BlockSpec((1, 1, D_MODEL), lambda i: (l, 0, 0)),
            *_mod_specs(l, SCALE2, 1),
            *_mod_specs(l, SHIFT2, 1),
        ],
        out_specs=[
            pl.BlockSpec((TM, D_MODEL), lambda i: (i, 0)),
            pl.BlockSpec((TM, D_MODEL), lambda i: (i, 0)),
        ],
        out_shape=[
            jax.ShapeDtypeStruct((t, D_MODEL), F32),
            jax.ShapeDtypeStruct((t, D_MODEL), BF16),
        ],
        compiler_params=_cparams(("arbitrary",)),
        name="out_proj",
    )(o_p, o_s, x, w_out_l, mod, mod, norm2_g, mod, mod, mod, mod)


N_DOWN_CHUNKS = 4


def _mlp_kernel(h_ref, x_ref, wu_ref, wd_ref, gts_ref, gtp_ref, fg_ref, *rest, nf, final,
                n_prompt_tiles):
    if final:
        yp_ref, ys_ref, acc_ref = rest
    else:
        xo_ref, acc_ref = rest
    i = pl.program_id(0)
    f = pl.program_id(1)

    @pl.when(f == 0)
    def _():
        acc_ref[...] = jnp.zeros_like(acc_ref)

    u = jnp.dot(h_ref[...], wu_ref[...], preferred_element_type=F32)
    u = jnp.square(jnp.maximum(u, 0.0)).astype(BF16)
    wc = D_MODEL // N_DOWN_CHUNKS
    for c in range(N_DOWN_CHUNKS):
        cols = pl.ds(c * wc, wc)
        acc_ref[:, cols] += jnp.dot(u, wd_ref[:, cols], preferred_element_type=F32)

    def finish(dst_ref, is_sample):
        for gi in range(TM // GROUP):
            rows = pl.ds(gi * GROUP, GROUP)
            xn = x_ref[rows, :] + _mod_row(is_sample, gts_ref, gtp_ref, gi) * acc_ref[rows, :]
            if final:
                xn = _rms(xn) * fg_ref[...]
            dst_ref[rows, :] = xn

    @pl.when(f == nf - 1)
    def _():
        if final:
            @pl.when(i < n_prompt_tiles)
            def _():
                finish(yp_ref, False)

            @pl.when(i >= n_prompt_tiles)
            def _():
                finish(ys_ref, True)
        else:
            finish(xo_ref, i >= n_prompt_tiles)


def _mlp_call(h, x, mod, w_up_l, w_down_l, final_g, l, final, n_prompt, tf=1024):
    t = x.shape[0]
    nf = D_FF // tf
    npt = n_prompt // TM
    kern = functools.partial(_mlp_kernel, nf=nf, final=final, n_prompt_tiles=npt)
    if final:
        last_p = npt - 1
        out_specs = [pl.BlockSpec((TM, D_MODEL), lambda i, f: (jnp.minimum(i, last_p), 0)),
                     pl.BlockSpec((TM, D_MODEL), lambda i, f: (jnp.maximum(i - npt, 0), 0))]
        out_shape = [jax.ShapeDtypeStruct((n_prompt, D_MODEL), F32),
                     jax.ShapeDtypeStruct((t - n_prompt, D_MODEL), F32)]
    else:
        out_specs = pl.BlockSpec((TM, D_MODEL), lambda i, f: (i, 0))
        out_shape = jax.ShapeDtypeStruct((t, D_MODEL), F32)
    return pl.pallas_call(
        kern,
        grid=(t // TM, nf),
        in_specs=[
            pl.BlockSpec((TM, D_MODEL), lambda i, f: (i, 0)),
            pl.BlockSpec((TM, D_MODEL), lambda i, f: (i, 0)),
            pl.BlockSpec((D_MODEL, tf), lambda i, f: (0, f)),
            pl.BlockSpec((tf, D_MODEL), lambda i, f: (f, 0)),
            *_mod_specs(l, GATE2, 2),
            pl.BlockSpec((1, D_MODEL), lambda i, f: (0, 0)),
        ],
        out_specs=out_specs,
        out_shape=out_shape,
        scratch_shapes=[pltpu.VMEM((TM, D_MODEL), F32)],
        compiler_params=_cparams(("arbitrary", "arbitrary")),
        name="mlp_final" if final else "mlp",
    )(h, x, w_up_l, w_down_l, mod, mod, final_g)


def _rope_tables(pos):
    half = DK_B // 2
    inv_freq = 1.0 / (ROPE_BASE ** jnp.linspace(0.0, 1.0, half, dtype=F32))
    ang = pos[:, None] * inv_freq[None, :]
    return jnp.cos(ang), jnp.sin(ang)


def kernel(x_prompt, x_sample, state_hgrn, state_ret, c_prompt, c_sample, lb_logits, w_ada, b_ada,
           norm1_g, norm2_g, w_in, hgrn_norm_g, ret_norm_g, w_out, w_up, w_down, final_g):
    bp, lp, _ = x_prompt.shape
    bs, ls, _ = x_sample.shape
    assert bp == 1 and ls == GROUP and bs == N_SEQ_S and lp % TM == 0
    n_prompt = bp * lp
    n_sample = bs * ls
    c_prompt_chunk = 256
    ca_prompt = 128

    x = jnp.concatenate([x_prompt.reshape(n_prompt, D_MODEL), x_sample.reshape(n_sample, D_MODEL)], 0)

    c_all = jnp.concatenate(
        [c_sample, c_prompt, jnp.zeros((MOD_ROWS - bs - bp, D_MODEL), F32)], 0)
    mod = _ada_call(c_all, w_ada, b_ada).reshape(DEPTH, MOD_ROWS, 1, N_MOD * D_MODEL)

    p = jax.nn.softmax(lb_logits.astype(F32), axis=0)
    cs = jnp.cumsum(p, axis=0)
    lb_all = (cs - cs[0:1]).reshape(DEPTH, H_A, 1, LANES)
    log_gamma = jnp.log1p(-jnp.exp2(-5.0 - jnp.arange(H_B, dtype=F32)))
    ghn = hgrn_norm_g.reshape(DEPTH, H_A, 1, LANES)
    grn = ret_norm_g.reshape(DEPTH, H_B, 1, DK_B)
    n1 = norm1_g.reshape(DEPTH, 1, D_MODEL)
    n2 = norm2_g.reshape(DEPTH, 1, D_MODEL)
    fg = final_g.reshape(1, D_MODEL)

    cos_p, sin_p = _rope_tables(jnp.arange(lp, dtype=F32))
    cos_s, sin_s = _rope_tables(PAST_LEN + jnp.arange(ls, dtype=F32))
    tab_p = _mixer_tables(c_prompt_chunk, ca_prompt, log_gamma)
    tab_s = _mixer_tables(ls, ls, log_gamma)

    w_in_l = w_in[0].astype(BF16)
    new_p = None
    new_s = None
    for l in range(DEPTH):
        proj = _inproj_call(x, mod, n1, w_in_l, l, n_prompt)
        casts = [(w_out, l), (w_up, l), (w_down, l)] + ([(w_in, l + 1)] if l + 1 < DEPTH else [])
        res = _mixer_call(proj, cos_p, sin_p, lb_all[l], ghn[l], grn[l], tab_p, None, new_p, casts,
                          c=c_prompt_chunk, ca=ca_prompt, batch=bp,
                          n_chunks=lp // c_prompt_chunk, row0=0, l=l)
        o_p, w_out_l, w_up_l, w_down_l = res[0], res[3], res[4], res[5]
        new_p = (res[1], res[2])
        if l + 1 < DEPTH:
            w_in_l = res[6]
        res = _mixer_call(proj, cos_s, sin_s, lb_all[l], ghn[l], grn[l], tab_s,
                          (state_hgrn, state_ret), new_s, [],
                          c=ls, ca=ls, batch=bs, n_chunks=1, row0=n_prompt, l=l)
        o_s = res[0]
        new_s = (res[1], res[2])
        x, h2 = _outproj_call(o_p, o_s, x, mod, n2, w_out_l, l, n_prompt)
        x = _mlp_call(h2, x, mod, w_up_l, w_down_l, fg, l, l == DEPTH - 1, n_prompt)

    y_p, y_s = x
    return (y_p.reshape(bp, lp, D_MODEL), y_s.reshape(bs, ls, D_MODEL),
            new_p[0], new_p[1], new_s[0], new_s[1])
```

```python
import functools

import numpy as np
import jax
import jax.numpy as jnp
from jax import lax
from jax.experimental import pallas as pl
from jax.experimental.pallas import tpu as pltpu

F32 = jnp.float32
BF16 = jnp.bfloat16

D_MODEL = 2048
DEPTH = 4
H_A = 8
H_B = 4
DK_B = 256
N_PROJ = 8192
N_MOD = 6
D_FF = 4 * D_MODEL
EPS = 1e-6
ROPE_BASE = 10000.0
LOG2_E = 1.4426950408889634
PAST_LEN = 2048

LANES = 128
N_PBLK = N_PROJ // LANES
N_OBLK = D_MODEL // LANES
GROUP = 64
TM = 512
N_SEQ_S = TM // GROUP
MOD_ROWS = 16
HGRN_GROUP = 4

VMEM_LIMIT = 56 * 1024 * 1024

_NT = (((1,), (1,)), ((), ()))
_TN = (((0,), (0,)), ((), ()))

SHIFT1, SCALE1, GATE1, SHIFT2, SCALE2, GATE2 = range(N_MOD)


def _cparams(sem):
    return pltpu.CompilerParams(dimension_semantics=sem, vmem_limit_bytes=VMEM_LIMIT)


def _silu(x):
    hx = 0.5 * x
    return hx * jnp.tanh(hx) + hx


def _rms(x):
    return x * lax.rsqrt(jnp.mean(x * x, axis=-1, keepdims=True) + EPS)


def _mod_specs(l, which, ngrid):
    if ngrid == 1:
        return [pl.BlockSpec((1, N_SEQ_S, 1, D_MODEL), lambda i: (l, 0, 0, which)),
                pl.BlockSpec((1, 1, 1, D_MODEL), lambda i: (l, N_SEQ_S, 0, which))]
    return [pl.BlockSpec((1, N_SEQ_S, 1, D_MODEL), lambda i, j: (l, 0, 0, which)),
            pl.BlockSpec((1, 1, 1, D_MODEL), lambda i, j: (l, N_SEQ_S, 0, which))]


def _mod_row(is_sample, s_ref, p_ref, gi):
    if isinstance(is_sample, bool):
        return s_ref[0, gi] if is_sample else p_ref[0, 0]
    return jnp.where(is_sample, s_ref[0, gi], p_ref[0, 0])


def _ada_kernel(c_ref, w_ref, b_ref, o_ref):
    c = _silu(c_ref[...]).astype(BF16)
    w = w_ref[0].astype(BF16)
    res = jnp.dot(c, w, preferred_element_type=F32) + b_ref[0]
    for r in range(MOD_ROWS):
        o_ref[0, r] = res[r:r + 1, :]


def _ada_call(c_all, w_ada, b_ada):
    tn = 1024
    n = N_MOD * D_MODEL
    return pl.pallas_call(
        _ada_kernel,
        grid=(DEPTH, n // tn),
        in_specs=[
            pl.BlockSpec((MOD_ROWS, D_MODEL), lambda l, j: (0, 0)),
            pl.BlockSpec((1, D_MODEL, tn), lambda l, j: (l, 0, j)),
            pl.BlockSpec((1, 1, tn), lambda l, j: (l, 0, j)),
        ],
        out_specs=pl.BlockSpec((1, MOD_ROWS, 1, tn), lambda l, j: (l, 0, 0, j)),
        out_shape=jax.ShapeDtypeStruct((DEPTH, MOD_ROWS, 1, n), F32),
        compiler_params=_cparams(("arbitrary", "arbitrary")),
        name="ada_mod",
    )(c_all, w_ada, b_ada.reshape(DEPTH, 1, n))


def _inproj_kernel(*refs, tn, n_prompt_tiles, first):
    if first:
        (xp_ref, xs_ref, g_ref, scs_ref, scp_ref, shs_ref, shp_ref, w_ref, o_ref, x_ref,
         r_scr) = refs
    else:
        x_ref, g_ref, scs_ref, scp_ref, shs_ref, shp_ref, w_ref, o_ref, r_scr = refs
    is_sample = pl.program_id(0) >= n_prompt_tiles

    @pl.when(pl.program_id(1) == 0)
    def _():
        if first:
            @pl.when(is_sample)
            def _():
                x_ref[...] = xs_ref[...]

            @pl.when(jnp.logical_not(is_sample))
            def _():
                x_ref[...] = xp_ref[...]
        x = x_ref[...]
        r_scr[...] = lax.rsqrt(jnp.mean(x * x, axis=-1, keepdims=True) + EPS)

    g = g_ref[0]
    hs = []
    for gi in range(TM // GROUP):
        rows = pl.ds(gi * GROUP, GROUP)
        gain = g * (1.0 + _mod_row(is_sample, scs_ref, scp_ref, gi))
        shift = _mod_row(is_sample, shs_ref, shp_ref, gi)
        hs.append((x_ref[rows, :] * r_scr[rows, :] * gain + shift).astype(BF16))
    h = jnp.concatenate(hs, axis=0)
    acc = jnp.dot(h, w_ref[...], preferred_element_type=F32)
    for c in range(tn // LANES):
        o_ref[c] = acc[:, c * LANES:(c + 1) * LANES].astype(BF16)


def _inproj_call(xs, mod, norm1_g, w_in_l, l, n_prompt, tn=2048):
    first = len(xs) == 2
    t = sum(x.shape[0] for x in xs)
    npt = n_prompt // TM
    kern = functools.partial(_inproj_kernel, tn=tn, n_prompt_tiles=npt, first=first)
    tile = pl.BlockSpec((TM, D_MODEL), lambda i, j: (i, 0))
    if first:
        x_specs = [pl.BlockSpec((TM, D_MODEL), lambda i, j: (jnp.minimum(i, npt - 1), 0)),
                   pl.BlockSpec((TM, D_MODEL), lambda i, j: (jnp.maximum(i - npt, 0), 0))]
    else:
        x_specs = [tile]
    out_specs = [pl.BlockSpec((tn // LANES, TM, LANES), lambda i, j: (j, i, 0))]
    out_shape = [jax.ShapeDtypeStruct((N_PBLK, t, LANES), BF16)]
    if first:
        out_specs.append(tile)
        out_shape.append(jax.ShapeDtypeStruct((t, D_MODEL), F32))
    return pl.pallas_call(
        kern,
        grid=(t // TM, N_PROJ // tn),
        in_specs=[
            *x_specs,
            pl.BlockSpec((1, 1, D_MODEL), lambda i, j: (l, 0, 0)),
            *_mod_specs(l, SCALE1, 2),
            *_mod_specs(l, SHIFT1, 2),
            pl.BlockSpec((D_MODEL, tn), lambda i, j: (0, j)),
        ],
        out_specs=out_specs,
        out_shape=out_shape,
        scratch_shapes=[pltpu.VMEM((TM, 1), F32)],
        compiler_params=_cparams(("arbitrary", "arbitrary")),
        name="in_proj",
    )(*xs, norm1_g, mod, mod, mod, mod, w_in_l)


def _hgrn_ref_rows(b_scr, lvl, c):
    half = 1 << lvl
    m = 2 * half
    if half >= 8:
        blocks = [jnp.broadcast_to(b_scr[pl.ds(j * m + half - 1, 1), :], (m, LANES))
                  for j in range(c // m)]
    elif half == 4:
        blocks = [jnp.broadcast_to(b_scr[pl.ds(8 * j + 3, 1), :], (8, LANES))
                  for j in range(c // 8)]
    else:
        sub = lax.broadcasted_iota(jnp.int32, (8, LANES), 0)
        blocks = [jnp.where(sub < 4,
                            jnp.broadcast_to(b_scr[pl.ds(8 * j + 1, 1), :], (8, LANES)),
                            jnp.broadcast_to(b_scr[pl.ds(8 * j + 5, 1), :], (8, LANES)))
                  for j in range(c // 8)]
    return blocks[0] if len(blocks) == 1 else jnp.concatenate(blocks, axis=0)


def _pick_rows(row, q, k, lvl, c):
    half = 1 << lvl
    if half < 8:
        return jnp.where((row & half) != 0, q, k)
    return jnp.concatenate([(q if j & 1 else k)[j * half:(j + 1) * half]
                            for j in range(c // half)], axis=0)


def _mixer_kernel(*refs, c, ca, n_chunks, has_state, n_alias, n_cast):
    refs = list(refs)
    (proj_ref, cos_ref, sin_ref, lb_ref, ghn_ref, grn_ref, lv_ref, tri_ref, dmat_ref, xi_ref,
     zeta_ref, gc_ref) = refs[:12]
    pos = 12
    if has_state:
        sa_in_ref, sb_in_ref = refs[pos:pos + 2]
        pos += 2
    pos += n_alias
    cast_in = refs[pos:pos + n_cast]
    pos += n_cast
    o_ref, sa_out_ref, sb_out_ref = refs[pos:pos + 3]
    pos += 3
    cast_out = refs[pos:pos + n_cast]
    pos += n_cast
    st_scr, sb_scr, b_scr = refs[pos:]

    ci = pl.program_id(1)
    n_lvl = ca.bit_length() - 1

    @pl.when(ci == 0)
    def _():
        if has_state:
            for h in range(H_A):
                st_scr[h] = sa_in_ref[0, 0, h].T
            sb_scr[...] = sb_in_ref[0, 0]
        else:
            st_scr[...] = jnp.zeros_like(st_scr)
            sb_scr[...] = jnp.zeros_like(sb_scr)

    for src, dst in zip(cast_in, cast_out):
        dst[...] = src[0].astype(BF16)

    row = lax.broadcasted_iota(jnp.int32, (ca, LANES), 0)

    n_sub = c // ca

    def hgrn_group(heads):
        streams = [(h, sub) for h in heads for sub in range(n_sub)]
        vals = []
        for h, sub in streams:
            rows = pl.ds(sub * ca, ca)
            lb = lb_ref[h]
            f = lb + (1.0 - lb) * jax.nn.sigmoid(proj_ref[H_A + h, rows, :].astype(F32))
            logf = jnp.log(f) * LOG2_E
            hi = logf.astype(BF16)
            lo = (logf - hi.astype(F32)).astype(BF16)
            b2 = jnp.dot(tri_ref[...], jnp.concatenate([hi, lo], axis=-1),
                         preferred_element_type=F32)
            vals.append(dict(f=f, b2=b2))

        for si, (h, sub) in enumerate(streams):
            d = vals[si]
            f, b2 = d["f"], d["b2"]
            q = _silu(proj_ref[h, pl.ds(sub * ca, ca), :].astype(F32))
            k = 1.0 - f
            b = b2[:, :LANES] + b2[:, LANES:]
            b_sub = b_scr.at[si]
            b_sub[...] = b
            z = jnp.where((row & 1) != 0, q * f, k).astype(BF16)
            p = lax.dot_general(z, z, _NT, preferred_element_type=F32)
            a = jnp.where(lv_ref[...] == 0, p.astype(BF16), jnp.zeros((), BF16))
            for lvl in range(1, n_lvl):
                e = jnp.exp2(-jnp.abs(b - _hgrn_ref_rows(b_sub, lvl, ca)))
                z = (_pick_rows(row, q, k, lvl, ca) * e).astype(BF16)
                p = lax.dot_general(z, z, _NT, preferred_element_type=F32)
                a = jnp.where(lv_ref[...] == lvl, p.astype(BF16), a)
            vb = proj_ref[2 * H_A + h, pl.ds(sub * ca, ca), :]
            b_last = b[ca - 1:ca, :]
            kd = (k * jnp.exp2(b_last - b)).astype(BF16)
            d.update(a=a, upd=lax.dot_general(vb, kd, _TN, preferred_element_type=F32),
                     decay=jnp.exp2(b_last), qi=(q * jnp.exp2(b)).astype(BF16),
                     diag=jnp.sum(q * k, axis=-1, keepdims=True))

        for gi, h in enumerate(heads):
            s_cur = st_scr[h]
            for sub in range(n_sub):
                d = vals[gi * n_sub + sub]
                vb = proj_ref[2 * H_A + h, pl.ds(sub * ca, ca), :]
                o = jnp.dot(d["a"], vb, preferred_element_type=F32)
                d["o"] = o + lax.dot_general(d["qi"], s_cur.astype(BF16), _NT,
                                             preferred_element_type=F32)
                s_cur = s_cur * d["decay"] + d["upd"]
            st_scr[h] = s_cur

        for si, (h, sub) in enumerate(streams):
            rows = pl.ds(sub * ca, ca)
            d = vals[si]
            o = d["o"] + d["diag"] * proj_ref[2 * H_A + h, rows, :].astype(F32)
            ga = proj_ref[3 * H_A + h, rows, :].astype(F32)
            o_ref[h, rows, :] = (_rms(o) * ghn_ref[h] * _silu(ga)).astype(BF16)

    for g0 in range(0, H_A, HGRN_GROUP):
        hgrn_group(list(range(g0, g0 + HGRN_GROUP)))

    cos = cos_ref[...]
    sin = sin_ref[...]

    rvals = []
    for hb in range(H_B):
        def pair(base):
            return (proj_ref[base + 2 * hb].astype(F32), proj_ref[base + 2 * hb + 1].astype(F32))

        q1, q2 = pair(4 * H_A)
        k1, k2 = pair(4 * H_A + 2 * H_B)
        vbase = 4 * H_A + 4 * H_B
        vb = jnp.concatenate([proj_ref[vbase + 2 * hb], proj_ref[vbase + 2 * hb + 1]], axis=-1)
        q = jnp.concatenate([q1 * cos - q2 * sin, q2 * cos + q1 * sin], axis=-1)
        k = jnp.concatenate([k1 * cos - k2 * sin, k2 * cos + k1 * sin], axis=-1) * (DK_B ** -0.5)
        qb = q.astype(BF16)
        s = sb_scr[hb]
        qk = lax.dot_general(qb, k.astype(BF16), _NT, preferred_element_type=F32)
        qs = jnp.dot(qb, s.astype(BF16), preferred_element_type=F32)
        kz = (k * zeta_ref[hb]).astype(BF16)
        sb_scr[hb] = gc_ref[hb] * s + lax.dot_general(kz, vb, _TN, preferred_element_type=F32)
        rvals.append((qk, qs, vb))
    for hb in range(H_B):
        qk, qs, vb = rvals[hb]
        o = jnp.dot((qk * dmat_ref[hb]).astype(BF16), vb, preferred_element_type=F32)
        o = o + qs * xi_ref[hb]
        gbase = 4 * H_A + 6 * H_B
        gate = jnp.concatenate([proj_ref[gbase + 2 * hb].astype(F32),
                                proj_ref[gbase + 2 * hb + 1].astype(F32)], axis=-1)
        out = _rms(o) * grn_ref[hb] * _silu(gate)
        o_ref[H_A + 2 * hb] = out[:, :LANES].astype(BF16)
        o_ref[H_A + 2 * hb + 1] = out[:, LANES:].astype(BF16)

    @pl.when(ci == n_chunks - 1)
    def _():
        for h in range(H_A):
            sa_out_ref[0, 0, h] = st_scr[h].T
        sb_out_ref[0, 0] = sb_scr[...]


def _mixer_tables(c, ca, log_gamma):
    t = np.arange(ca)
    x = t[:, None] ^ t[None, :]
    lv = np.where(t[:, None] > t[None, :], np.floor(np.log2(np.maximum(x, 1))), -1).astype(np.int32)
    tri = (t[:, None] >= t[None, :]).astype(np.float32)
    idx = jnp.arange(c, dtype=F32)
    lg = log_gamma[:, None, None]
    causal = jnp.asarray(np.tril(np.ones((c, c), dtype=bool)))
    dmat = jnp.where(causal[None], jnp.exp((idx[:, None] - idx[None, :])[None] * lg), 0.0)
    xi = jnp.exp((idx + 1.0)[None, :] * log_gamma[:, None])[..., None]
    zeta = jnp.exp((c - 1.0 - idx)[None, :] * log_gamma[:, None])[..., None]
    g_c = jnp.exp(c * log_gamma)[:, None, None]
    return dict(
        lv=jnp.asarray(lv, dtype=BF16), tri=jnp.asarray(tri, dtype=BF16), dmat=dmat,
        xi=jnp.broadcast_to(xi, (H_B, c, DK_B)), zeta=jnp.broadcast_to(zeta, (H_B, c, DK_B)),
        gc=jnp.broadcast_to(g_c, (H_B, 1, DK_B)))


def _mixer_call(proj, cos, sin, lb, ghn, grn, tab, states, prev_states, casts, *, c, ca, batch,
                n_chunks, row0, l):
    has_state = states is not None
    blk0 = row0 // c
    const3 = lambda b, i: (0, 0, 0)
    sa_spec = pl.BlockSpec((1, 1, H_A, LANES, LANES), lambda b, i: (l, b, 0, 0, 0))
    sb_spec = pl.BlockSpec((1, 1, H_B, DK_B, DK_B), lambda b, i: (l, b, 0, 0, 0))
    in_specs = [
        pl.BlockSpec((N_PBLK, c, LANES), lambda b, i: (0, blk0 + b * n_chunks + i, 0)),
        pl.BlockSpec((c, LANES), lambda b, i: (i, 0)),
        pl.BlockSpec((c, LANES), lambda b, i: (i, 0)),
        pl.BlockSpec((H_A, 1, LANES), const3),
        pl.BlockSpec((H_A, 1, LANES), const3),
        pl.BlockSpec((H_B, 1, DK_B), const3),
        pl.BlockSpec((ca, ca), lambda b, i: (0, 0)),
        pl.BlockSpec((ca, ca), lambda b, i: (0, 0)),
        pl.BlockSpec((H_B, c, c), const3),
        pl.BlockSpec((H_B, c, DK_B), const3),
        pl.BlockSpec((H_B, c, DK_B), const3),
        pl.BlockSpec((H_B, 1, DK_B), const3),
    ]
    args = [proj, cos, sin, lb, ghn, grn, tab["lv"], tab["tri"], tab["dmat"], tab["xi"],
            tab["zeta"], tab["gc"]]
    if has_state:
        in_specs += [sa_spec, sb_spec]
        args += list(states)
    aliases = {}
    if prev_states is not None:
        for k, arr in enumerate(prev_states):
            aliases[len(args)] = 1 + k
            in_specs.append(pl.BlockSpec(memory_space=pl.ANY))
            args.append(arr)
    out_specs = [
        pl.BlockSpec((N_OBLK, c, LANES), lambda b, i: (0, b * n_chunks + i, 0)), sa_spec, sb_spec]
    out_shape = [
        jax.ShapeDtypeStruct((N_OBLK, batch * n_chunks * c, LANES), BF16),
        jax.ShapeDtypeStruct((DEPTH, batch, H_A, LANES, LANES), F32),
        jax.ShapeDtypeStruct((DEPTH, batch, H_B, DK_B, DK_B), F32),
    ]
    assert not casts or batch == 1
    for w, layer in casts:
        _, r, cols = w.shape
        rb = r // n_chunks
        in_specs.append(pl.BlockSpec((1, rb, cols), lambda b, i, layer=layer: (layer, i, 0)))
        args.append(w)
        out_specs.append(pl.BlockSpec((rb, cols), lambda b, i: (i, 0)))
        out_shape.append(jax.ShapeDtypeStruct((r, cols), BF16))
    kern = functools.partial(_mixer_kernel, c=c, ca=ca, n_chunks=n_chunks, has_state=has_state,
                             n_alias=len(aliases), n_cast=len(casts))
    return pl.pallas_call(
        kern,
        grid=(batch, n_chunks),
        in_specs=in_specs,
        out_specs=out_specs,
        out_shape=out_shape,
        input_output_aliases=aliases,
        scratch_shapes=[
            pltpu.VMEM((H_A, LANES, LANES), F32),
            pltpu.VMEM((H_B, DK_B, DK_B), F32),
            pltpu.VMEM((HGRN_GROUP * (c // ca), ca, LANES), F32),
        ],
        compiler_params=_cparams(("arbitrary", "arbitrary")),
        name="mixer_state" if has_state else "mixer_prompt",
    )(*args)


def _outproj_kernel(op_ref, os_ref, x_ref, w_ref, gts_ref, gtp_ref, g2_ref, scs_ref, scp_ref,
                    shs_ref, shp_ref, xo_ref, h_ref, *, n_prompt_tiles):
    i = pl.program_id(0)
    half = TM // 2

    def run(o_ref, is_sample):
        for hf in range(2):
            o = jnp.concatenate([o_ref[cb, pl.ds(hf * half, half), :] for cb in range(N_OBLK)],
                                axis=-1)
            y = jnp.dot(o, w_ref[...], preferred_element_type=F32)
            for gl in range(half // GROUP):
                gi = hf * (half // GROUP) + gl
                rows = pl.ds(gi * GROUP, GROUP)
                xn = (x_ref[rows, :] + _mod_row(is_sample, gts_ref, gtp_ref, gi)
                      * y[gl * GROUP:(gl + 1) * GROUP, :])
                xo_ref[rows, :] = xn
                gain = g2_ref[0] * (1.0 + _mod_row(is_sample, scs_ref, scp_ref, gi))
                h_ref[rows, :] = (_rms(xn) * gain
                                  + _mod_row(is_sample, shs_ref, shp_ref, gi)).astype(BF16)

    @pl.when(i < n_prompt_tiles)
    def _():
        run(op_ref, False)

    @pl.when(i >= n_prompt_tiles)
    def _():
        run(os_ref, True)


def _outproj_call(o_p, o_s, x, mod, norm2_g, w_out_l, l, n_prompt):
    t = x.shape[0]
    npt = n_prompt // TM
    kern = functools.partial(_outproj_kernel, n_prompt_tiles=npt)
    last_p = npt - 1
    return pl.pallas_call(
        kern,
        grid=(t // TM,),
        in_specs=[
            pl.BlockSpec((N_OBLK, TM, LANES), lambda i: (0, jnp.minimum(i, last_p), 0)),
            pl.BlockSpec((N_OBLK, TM, LANES), lambda i: (0, jnp.maximum(i - npt, 0), 0)),
            pl.BlockSpec((TM, D_MODEL), lambda i: (i, 0)),
            pl.BlockSpec((D_MODEL, D_MODEL), lambda i: (0, 0)),
            *_mod_specs(l, GATE1, 1),
            pl.BlockSpec((1, 1, D_MODEL), lambda i: (l, 0, 0)),
            *_mod_specs(l, SCALE2, 1),
            *_mod_specs(l, SHIFT2, 1),
        ],
        out_specs=[
            pl.BlockSpec((TM, D_MODEL), lambda i: (i, 0)),
            pl.BlockSpec((TM, D_MODEL), lambda i: (i, 0)),
        ],
        out_shape=[
            jax.ShapeDtypeStruct((t, D_MODEL), F32),
            jax.ShapeDtypeStruct((t, D_MODEL), BF16),
        ],
        compiler_params=_cparams(("arbitrary",)),
        name="out_proj",
    )(o_p, o_s, x, w_out_l, mod, mod, norm2_g, mod, mod, mod, mod)


def _mlp_kernel(h_ref, x_ref, wu_ref, wd_ref, gts_ref, gtp_ref, fg_ref, *rest, nf, final,
                n_prompt_tiles):
    if final:
        yp_ref, ys_ref, acc_ref = rest
    else:
        xo_ref, acc_ref = rest
    i = pl.program_id(0)
    f = pl.program_id(1)

    @pl.when(f == 0)
    def _():
        acc_ref[...] = jnp.zeros_like(acc_ref)

    u = jnp.dot(h_ref[...], wu_ref[...], preferred_element_type=F32)
    u = jnp.square(jnp.maximum(u, 0.0)).astype(BF16)
    acc_ref[...] += jnp.dot(u, wd_ref[...], preferred_element_type=F32)

    def finish(dst_ref, is_sample):
        for gi in range(TM // GROUP):
            rows = pl.ds(gi * GROUP, GROUP)
            xn = x_ref[rows, :] + _mod_row(is_sample, gts_ref, gtp_ref, gi) * acc_ref[rows, :]
            if final:
                xn = _rms(xn) * fg_ref[...]
            dst_ref[rows, :] = xn

    @pl.when(f == nf - 1)
    def _():
        if final:
            @pl.when(i < n_prompt_tiles)
            def _():
                finish(yp_ref, False)

            @pl.when(i >= n_prompt_tiles)
            def _():
                finish(ys_ref, True)
        else:
            finish(xo_ref, i >= n_prompt_tiles)


def _mlp_call(h, x, mod, w_up_l, w_down_l, final_g, l, final, n_prompt, tf=1024):
    t = x.shape[0]
    nf = D_FF // tf
    npt = n_prompt // TM
    kern = functools.partial(_mlp_kernel, nf=nf, final=final, n_prompt_tiles=npt)
    if final:
        last_p = npt - 1
        out_specs = [pl.BlockSpec((TM, D_MODEL), lambda i, f: (jnp.minimum(i, last_p), 0)),
                     pl.BlockSpec((TM, D_MODEL), lambda i, f: (jnp.maximum(i - npt, 0), 0))]
        out_shape = [jax.ShapeDtypeStruct((n_prompt, D_MODEL), F32),
                     jax.ShapeDtypeStruct((t - n_prompt, D_MODEL), F32)]
    else:
        out_specs = pl.BlockSpec((TM, D_MODEL), lambda i, f: (i, 0))
        out_shape = jax.ShapeDtypeStruct((t, D_MODEL), F32)
    return pl.pallas_call(
        kern,
        grid=(t // TM, nf),
        in_specs=[
            pl.BlockSpec((TM, D_MODEL), lambda i, f: (i, 0)),
            pl.BlockSpec((TM, D_MODEL), lambda i, f: (i, 0)),
            pl.BlockSpec((D_MODEL, tf), lambda i, f: (0, f)),
            pl.BlockSpec((tf, D_MODEL), lambda i, f: (f, 0)),
            *_mod_specs(l, GATE2, 2),
            pl.BlockSpec((1, D_MODEL), lambda i, f: (0, 0)),
        ],
        out_specs=out_specs,
        out_shape=out_shape,
        scratch_shapes=[pltpu.VMEM((TM, D_MODEL), F32)],
        compiler_params=_cparams(("arbitrary", "arbitrary")),
        name="mlp_final" if final else "mlp",
    )(h, x, w_up_l, w_down_l, mod, mod, final_g)


def _rope_tables(pos):
    half = DK_B // 2
    inv_freq = 1.0 / (ROPE_BASE ** jnp.linspace(0.0, 1.0, half, dtype=F32))
    ang = pos[:, None] * inv_freq[None, :]
    return jnp.cos(ang), jnp.sin(ang)


def kernel(x_prompt, x_sample, state_hgrn, state_ret, c_prompt, c_sample, lb_logits, w_ada, b_ada,
           norm1_g, norm2_g, w_in, hgrn_norm_g, ret_norm_g, w_out, w_up, w_down, final_g):
    bp, lp, _ = x_prompt.shape
    bs, ls, _ = x_sample.shape
    assert bp == 1 and ls == GROUP and bs == N_SEQ_S and lp % TM == 0
    n_prompt = bp * lp
    n_sample = bs * ls
    c_prompt_chunk = 256
    ca_prompt = 128

    c_all = jnp.concatenate(
        [c_sample, c_prompt, jnp.zeros((MOD_ROWS - bs - bp, D_MODEL), F32)], 0)
    mod = _ada_call(c_all, w_ada, b_ada)

    p = jax.nn.softmax(lb_logits.astype(F32), axis=0)
    cs = jnp.cumsum(p, axis=0)
    lb_all = (cs - cs[0:1]).reshape(DEPTH, H_A, 1, LANES)
    log_gamma = jnp.log1p(-jnp.exp2(-5.0 - jnp.arange(H_B, dtype=F32)))
    ghn = hgrn_norm_g.reshape(DEPTH, H_A, 1, LANES)
    grn = ret_norm_g.reshape(DEPTH, H_B, 1, DK_B)
    n1 = norm1_g.reshape(DEPTH, 1, D_MODEL)
    n2 = norm2_g.reshape(DEPTH, 1, D_MODEL)
    fg = final_g.reshape(1, D_MODEL)

    cos_p, sin_p = _rope_tables(jnp.arange(lp, dtype=F32))
    cos_s, sin_s = _rope_tables(PAST_LEN + jnp.arange(ls, dtype=F32))
    tab_p = _mixer_tables(c_prompt_chunk, ca_prompt, log_gamma)
    tab_s = _mixer_tables(ls, ls, log_gamma)

    w_in_l = w_in[0].astype(BF16)
    new_p = None
    new_s = None
    for l in range(DEPTH):
        if l == 0:
            proj, x = _inproj_call((x_prompt.reshape(n_prompt, D_MODEL),
                                    x_sample.reshape(n_sample, D_MODEL)),
                                   mod, n1, w_in_l, l, n_prompt)
        else:
            proj, = _inproj_call((x,), mod, n1, w_in_l, l, n_prompt)
        casts = [(w_out, l), (w_up, l), (w_down, l)] + ([(w_in, l + 1)] if l + 1 < DEPTH else [])
        res = _mixer_call(proj, cos_p, sin_p, lb_all[l], ghn[l], grn[l], tab_p, None, new_p, casts,
                          c=c_prompt_chunk, ca=ca_prompt, batch=bp,
                          n_chunks=lp // c_prompt_chunk, row0=0, l=l)
        o_p, w_out_l, w_up_l, w_down_l = res[0], res[3], res[4], res[5]
        new_p = (res[1], res[2])
        if l + 1 < DEPTH:
            w_in_l = res[6]
        res = _mixer_call(proj, cos_s, sin_s, lb_all[l], ghn[l], grn[l], tab_s,
                          (state_hgrn, state_ret), new_s, [],
                          c=ls, ca=ls, batch=bs, n_chunks=1, row0=n_prompt, l=l)
        o_s = res[0]
        new_s = (res[1], res[2])
        x, h2 = _outproj_call(o_p, o_s, x, mod, n2, w_out_l, l, n_prompt)
        x = _mlp_call(h2, x, mod, w_up_l, w_down_l, fg, l, l == DEPTH - 1, n_prompt)

    y_p, y_s = x
    return (y_p.reshape(bp, lp, D_MODEL), y_s.reshape(bs, ls, D_MODEL),
            new_p[0], new_p[1], new_s[0], new_s[1])
```

```python
import functools

import numpy as np
import jax
import jax.numpy as jnp
from jax import lax
from jax.experimental import pallas as pl
from jax.experimental.pallas import tpu as pltpu

F32 = jnp.float32
BF16 = jnp.bfloat16

D_MODEL = 2048
DEPTH = 4
H_A = 8
H_B = 4
DK_B = 256
N_PROJ = 8192
N_MOD = 6
D_FF = 4 * D_MODEL
EPS = 1e-6
ROPE_BASE = 10000.0
LOG2_E = 1.4426950408889634
PAST_LEN = 2048

LANES = 128
N_PBLK = N_PROJ // LANES
N_OBLK = D_MODEL // LANES
GROUP = 64
TM = 512
N_SEQ_S = TM // GROUP
MOD_ROWS = 16
HGRN_GROUP = 8

VMEM_LIMIT = 56 * 1024 * 1024

_NT = (((1,), (1,)), ((), ()))
_TN = (((0,), (0,)), ((), ()))

SHIFT1, SCALE1, GATE1, SHIFT2, SCALE2, GATE2 = range(N_MOD)


def _cparams(sem):
    return pltpu.CompilerParams(dimension_semantics=sem, vmem_limit_bytes=VMEM_LIMIT)


def _silu(x):
    hx = 0.5 * x
    return hx * jnp.tanh(hx) + hx


def _rms(x):
    return x * lax.rsqrt(jnp.mean(x * x, axis=-1, keepdims=True) + EPS)


def _mod_specs(l, which, ngrid):
    if ngrid == 1:
        return [pl.BlockSpec((1, N_SEQ_S, 1, D_MODEL), lambda i: (l, 0, 0, which)),
                pl.BlockSpec((1, 1, 1, D_MODEL), lambda i: (l, N_SEQ_S, 0, which))]
    return [pl.BlockSpec((1, N_SEQ_S, 1, D_MODEL), lambda i, j: (l, 0, 0, which)),
            pl.BlockSpec((1, 1, 1, D_MODEL), lambda i, j: (l, N_SEQ_S, 0, which))]


def _mod_row(is_sample, s_ref, p_ref, gi):
    if isinstance(is_sample, bool):
        return s_ref[0, gi] if is_sample else p_ref[0, 0]
    return jnp.where(is_sample, s_ref[0, gi], p_ref[0, 0])


def _ada_kernel(c_ref, w_ref, b_ref, o_ref):
    c = _silu(c_ref[...]).astype(BF16)
    w = w_ref[0].astype(BF16)
    res = jnp.dot(c, w, preferred_element_type=F32) + b_ref[0]
    for r in range(MOD_ROWS):
        o_ref[0, r] = res[r:r + 1, :]


def _ada_call(c_all, w_ada, b_ada):
    tn = 1024
    n = N_MOD * D_MODEL
    return pl.pallas_call(
        _ada_kernel,
        grid=(DEPTH, n // tn),
        in_specs=[
            pl.BlockSpec((MOD_ROWS, D_MODEL), lambda l, j: (0, 0)),
            pl.BlockSpec((1, D_MODEL, tn), lambda l, j: (l, 0, j)),
            pl.BlockSpec((1, 1, tn), lambda l, j: (l, 0, j)),
        ],
        out_specs=pl.BlockSpec((1, MOD_ROWS, 1, tn), lambda l, j: (l, 0, 0, j)),
        out_shape=jax.ShapeDtypeStruct((DEPTH, MOD_ROWS, 1, n), F32),
        compiler_params=_cparams(("arbitrary", "arbitrary")),
        name="ada_mod",
    )(c_all, w_ada, b_ada.reshape(DEPTH, 1, n))


def _inproj_tile(x_ref, r_ref, g_ref, scs_ref, scp_ref, shs_ref, shp_ref, w_ref, o_ref, is_sample,
                 tn):
    g = g_ref[0]
    hs = []
    for gi in range(TM // GROUP):
        rows = pl.ds(gi * GROUP, GROUP)
        gain = g * (1.0 + _mod_row(is_sample, scs_ref, scp_ref, gi))
        shift = _mod_row(is_sample, shs_ref, shp_ref, gi)
        hs.append((x_ref[rows, :] * r_ref[rows, :] * gain + shift).astype(BF16))
    h = jnp.concatenate(hs, axis=0)
    acc = jnp.dot(h, w_ref[...], preferred_element_type=F32)
    for c in range(tn // LANES):
        o_ref[c] = acc[:, c * LANES:(c + 1) * LANES].astype(BF16)


def _inproj_first_kernel(xp_ref, xs_ref, g_ref, scs_ref, scp_ref, shs_ref, shp_ref, w_ref, o_ref,
                         x_ref, r_scr, *, tn, n_prompt_tiles):
    is_sample = pl.program_id(0) >= n_prompt_tiles

    @pl.when(pl.program_id(1) == 0)
    def _():
        @pl.when(is_sample)
        def _():
            x_ref[...] = xs_ref[...]

        @pl.when(jnp.logical_not(is_sample))
        def _():
            x_ref[...] = xp_ref[...]
        x = x_ref[...]
        r_scr[...] = lax.rsqrt(jnp.mean(x * x, axis=-1, keepdims=True) + EPS)

    _inproj_tile(x_ref, r_scr, g_ref, scs_ref, scp_ref, shs_ref, shp_ref, w_ref, o_ref, is_sample,
                 tn)


def _inproj_kernel(x_ref, r_ref, g_ref, scs_ref, scp_ref, shs_ref, shp_ref, w_ref, o_ref,
                   *, tn, n_prompt_tiles):
    is_sample = pl.program_id(1) >= n_prompt_tiles
    _inproj_tile(x_ref, r_ref, g_ref, scs_ref, scp_ref, shs_ref, shp_ref, w_ref, o_ref, is_sample,
                 tn)


def _inproj_first_call(x_prompt, x_sample, mod, norm1_g, w_in_l, l, tn=2048):
    n_prompt = x_prompt.shape[0]
    t = n_prompt + x_sample.shape[0]
    npt = n_prompt // TM
    kern = functools.partial(_inproj_first_kernel, tn=tn, n_prompt_tiles=npt)
    tile = pl.BlockSpec((TM, D_MODEL), lambda i, j: (i, 0))
    return pl.pallas_call(
        kern,
        grid=(t // TM, N_PROJ // tn),
        in_specs=[
            pl.BlockSpec((TM, D_MODEL), lambda i, j: (jnp.minimum(i, npt - 1), 0)),
            pl.BlockSpec((TM, D_MODEL), lambda i, j: (jnp.maximum(i - npt, 0), 0)),
            pl.BlockSpec((1, 1, D_MODEL), lambda i, j: (l, 0, 0)),
            *_mod_specs(l, SCALE1, 2),
            *_mod_specs(l, SHIFT1, 2),
            pl.BlockSpec((D_MODEL, tn), lambda i, j: (0, j)),
        ],
        out_specs=[pl.BlockSpec((tn // LANES, TM, LANES), lambda i, j: (j, i, 0)), tile],
        out_shape=[jax.ShapeDtypeStruct((N_PBLK, t, LANES), BF16),
                   jax.ShapeDtypeStruct((t, D_MODEL), F32)],
        scratch_shapes=[pltpu.VMEM((TM, 1), F32)],
        compiler_params=_cparams(("arbitrary", "arbitrary")),
        name="in_proj_first",
    )(x_prompt, x_sample, norm1_g, mod, mod, mod, mod, w_in_l)


def _inproj_call(x, r, mod, norm1_g, w_in_l, l, n_prompt, tn=2048):
    t = x.shape[0]
    kern = functools.partial(_inproj_kernel, tn=tn, n_prompt_tiles=n_prompt // TM)
    return pl.pallas_call(
        kern,
        grid=(N_PROJ // tn, t // TM),
        in_specs=[
            pl.BlockSpec((TM, D_MODEL), lambda j, i: (i, 0)),
            pl.BlockSpec((TM, 1), lambda j, i: (i, 0)),
            pl.BlockSpec((1, 1, D_MODEL), lambda j, i: (l, 0, 0)),
            *_mod_specs(l, SCALE1, 2),
            *_mod_specs(l, SHIFT1, 2),
            pl.BlockSpec((D_MODEL, tn), lambda j, i: (0, j)),
        ],
        out_specs=pl.BlockSpec((tn // LANES, TM, LANES), lambda j, i: (j, i, 0)),
        out_shape=jax.ShapeDtypeStruct((N_PBLK, t, LANES), BF16),
        compiler_params=_cparams(("arbitrary", "arbitrary")),
        name="in_proj",
    )(x, r, norm1_g, mod, mod, mod, mod, w_in_l)


def _hgrn_ref_rows(b_scr, lvl, c):
    half = 1 << lvl
    m = 2 * half
    if half >= 8:
        blocks = [jnp.broadcast_to(b_scr[pl.ds(j * m + half - 1, 1), :], (m, LANES))
                  for j in range(c // m)]
    elif half == 4:
        blocks = [jnp.broadcast_to(b_scr[pl.ds(8 * j + 3, 1), :], (8, LANES))
                  for j in range(c // 8)]
    else:
        sub = lax.broadcasted_iota(jnp.int32, (8, LANES), 0)
        blocks = [jnp.where(sub < 4,
                            jnp.broadcast_to(b_scr[pl.ds(8 * j + 1, 1), :], (8, LANES)),
                            jnp.broadcast_to(b_scr[pl.ds(8 * j + 5, 1), :], (8, LANES)))
                  for j in range(c // 8)]
    return blocks[0] if len(blocks) == 1 else jnp.concatenate(blocks, axis=0)


def _pick_rows(row, q, k, lvl, c):
    half = 1 << lvl
    if half < 8:
        return jnp.where((row & half) != 0, q, k)
    return jnp.concatenate([(q if j & 1 else k)[j * half:(j + 1) * half]
                            for j in range(c // half)], axis=0)


def _mixer_kernel(*refs, c, ca, n_chunks, has_state, n_alias, n_cast):
    refs = list(refs)
    (proj_ref, cos_ref, sin_ref, lb_ref, ghn_ref, grn_ref, lv_ref, tri_ref, dmat_ref, xi_ref,
     zeta_ref, gc_ref) = refs[:12]
    pos = 12
    if has_state:
        sa_in_ref, sb_in_ref = refs[pos:pos + 2]
        pos += 2
    pos += n_alias
    cast_in = refs[pos:pos + n_cast]
    pos += n_cast
    o_ref, sa_out_ref, sb_out_ref = refs[pos:pos + 3]
    pos += 3
    cast_out = refs[pos:pos + n_cast]
    pos += n_cast
    st_scr, sb_scr, b_scr = refs[pos:]

    ci = pl.program_id(1)
    n_lvl = ca.bit_length() - 1

    @pl.when(ci == 0)
    def _():
        if has_state:
            for h in range(H_A):
                st_scr[h] = sa_in_ref[0, 0, h].T
            sb_scr[...] = sb_in_ref[0, 0]
        else:
            st_scr[...] = jnp.zeros_like(st_scr)
            sb_scr[...] = jnp.zeros_like(sb_scr)

    for src, dst in zip(cast_in, cast_out):
        dst[...] = src[0].astype(BF16)

    row = lax.broadcasted_iota(jnp.int32, (ca, LANES), 0)

    n_sub = c // ca

    def hgrn_group(heads):
        streams = [(h, sub) for h in heads for sub in range(n_sub)]
        vals = []
        for h, sub in streams:
            rows = pl.ds(sub * ca, ca)
            lb = lb_ref[h]
            f = lb + (1.0 - lb) * jax.nn.sigmoid(proj_ref[H_A + h, rows, :].astype(F32))
            logf = jnp.log(f) * LOG2_E
            hi = logf.astype(BF16)
            lo = (logf - hi.astype(F32)).astype(BF16)
            b2 = jnp.dot(tri_ref[...], jnp.concatenate([hi, lo], axis=-1),
                         preferred_element_type=F32)
            vals.append(dict(f=f, b2=b2))

        for si, (h, sub) in enumerate(streams):
            d = vals[si]
            f, b2 = d["f"], d["b2"]
            q = _silu(proj_ref[h, pl.ds(sub * ca, ca), :].astype(F32))
            k = 1.0 - f
            b = b2[:, :LANES] + b2[:, LANES:]
            b_sub = b_scr.at[si]
            b_sub[...] = b
            z = jnp.where((row & 1) != 0, q * f, k).astype(BF16)
            p = lax.dot_general(z, z, _NT, preferred_element_type=F32)
            a = jnp.where(lv_ref[...] == 0, p.astype(BF16), jnp.zeros((), BF16))
            for lvl in range(1, n_lvl):
                e = jnp.exp2(-jnp.abs(b - _hgrn_ref_rows(b_sub, lvl, ca)))
                z = (_pick_rows(row, q, k, lvl, ca) * e).astype(BF16)
                p = lax.dot_general(z, z, _NT, preferred_element_type=F32)
                a = jnp.where(lv_ref[...] == lvl, p.astype(BF16), a)
            vb = proj_ref[2 * H_A + h, pl.ds(sub * ca, ca), :]
            b_last = b[ca - 1:ca, :]
            kd = (k * jnp.exp2(b_last - b)).astype(BF16)
            d.update(a=a, upd=lax.dot_general(vb, kd, _TN, preferred_element_type=F32),
                     decay=jnp.exp2(b_last), qi=(q * jnp.exp2(b)).astype(BF16),
                     diag=jnp.sum(q * k, axis=-1, keepdims=True))

        for gi, h in enumerate(heads):
            s_cur = st_scr[h]
            for sub in range(n_sub):
                d = vals[gi * n_sub + sub]
                vb = proj_ref[2 * H_A + h, pl.ds(sub * ca, ca), :]
                o = jnp.dot(d["a"], vb, preferred_element_type=F32)
                d["o"] = o + lax.dot_general(d["qi"], s_cur.astype(BF16), _NT,
                                             preferred_element_type=F32)
                s_cur = s_cur * d["decay"] + d["upd"]
            st_scr[h] = s_cur

        for si, (h, sub) in enumerate(streams):
            rows = pl.ds(sub * ca, ca)
            d = vals[si]
            o = d["o"] + d["diag"] * proj_ref[2 * H_A + h, rows, :].astype(F32)
            ga = proj_ref[3 * H_A + h, rows, :].astype(F32)
            o_ref[h, rows, :] = (_rms(o) * ghn_ref[h] * _silu(ga)).astype(BF16)

    for g0 in range(0, H_A, HGRN_GROUP):
        hgrn_group(list(range(g0, g0 + HGRN_GROUP)))

    cos = cos_ref[...]
    sin = sin_ref[...]

    rvals = []
    for hb in range(H_B):
        def pair(base):
            return (proj_ref[base + 2 * hb].astype(F32), proj_ref[base + 2 * hb + 1].astype(F32))

        q1, q2 = pair(4 * H_A)
        k1, k2 = pair(4 * H_A + 2 * H_B)
        vbase = 4 * H_A + 4 * H_B
        vb = jnp.concatenate([proj_ref[vbase + 2 * hb], proj_ref[vbase + 2 * hb + 1]], axis=-1)
        q = jnp.concatenate([q1 * cos - q2 * sin, q2 * cos + q1 * sin], axis=-1)
        k = jnp.concatenate([k1 * cos - k2 * sin, k2 * cos + k1 * sin], axis=-1) * (DK_B ** -0.5)
        qb = q.astype(BF16)
        s = sb_scr[hb]
        qk = lax.dot_general(qb, k.astype(BF16), _NT, preferred_element_type=F32)
        qs = jnp.dot(qb, s.astype(BF16), preferred_element_type=F32)
        kz = (k * zeta_ref[hb]).astype(BF16)
        sb_scr[hb] = gc_ref[hb] * s + lax.dot_general(kz, vb, _TN, preferred_element_type=F32)
        rvals.append((qk, qs, vb))
    for hb in range(H_B):
        qk, qs, vb = rvals[hb]
        o = jnp.dot((qk * dmat_ref[hb]).astype(BF16), vb, preferred_element_type=F32)
        o = o + qs * xi_ref[hb]
        gbase = 4 * H_A + 6 * H_B
        gate = jnp.concatenate([proj_ref[gbase + 2 * hb].astype(F32),
                                proj_ref[gbase + 2 * hb + 1].astype(F32)], axis=-1)
        out = _rms(o) * grn_ref[hb] * _silu(gate)
        o_ref[H_A + 2 * hb] = out[:, :LANES].astype(BF16)
        o_ref[H_A + 2 * hb + 1] = out[:, LANES:].astype(BF16)

    @pl.when(ci == n_chunks - 1)
    def _():
        for h in range(H_A):
            sa_out_ref[0, 0, h] = st_scr[h].T
        sb_out_ref[0, 0] = sb_scr[...]


def _mixer_tables(c, ca, log_gamma):
    t = np.arange(ca)
    x = t[:, None] ^ t[None, :]
    lv = np.where(t[:, None] > t[None, :], np.floor(np.log2(np.maximum(x, 1))), -1).astype(np.int32)
    tri = (t[:, None] >= t[None, :]).astype(np.float32)
    idx = jnp.arange(c, dtype=F32)
    lg = log_gamma[:, None, None]
    causal = jnp.asarray(np.tril(np.ones((c, c), dtype=bool)))
    dmat = jnp.where(causal[None], jnp.exp((idx[:, None] - idx[None, :])[None] * lg), 0.0)
    xi = jnp.exp((idx + 1.0)[None, :] * log_gamma[:, None])[..., None]
    zeta = jnp.exp((c - 1.0 - idx)[None, :] * log_gamma[:, None])[..., None]
    g_c = jnp.exp(c * log_gamma)[:, None, None]
    return dict(
        lv=jnp.asarray(lv, dtype=BF16), tri=jnp.asarray(tri, dtype=BF16), dmat=dmat,
        xi=jnp.broadcast_to(xi, (H_B, c, DK_B)), zeta=jnp.broadcast_to(zeta, (H_B, c, DK_B)),
        gc=jnp.broadcast_to(g_c, (H_B, 1, DK_B)))


def _mixer_call(proj, cos, sin, lb, ghn, grn, tab, states, prev_states, casts, *, c, ca, batch,
                n_chunks, row0, l):
    has_state = states is not None
    blk0 = row0 // c
    const3 = lambda b, i: (0, 0, 0)
    sa_spec = pl.BlockSpec((1, 1, H_A, LANES, LANES), lambda b, i: (l, b, 0, 0, 0))
    sb_spec = pl.BlockSpec((1, 1, H_B, DK_B, DK_B), lambda b, i: (l, b, 0, 0, 0))
    in_specs = [
        pl.BlockSpec((N_PBLK, c, LANES), lambda b, i: (0, blk0 + b * n_chunks + i, 0)),
        pl.BlockSpec((c, LANES), lambda b, i: (i, 0)),
        pl.BlockSpec((c, LANES), lambda b, i: (i, 0)),
        pl.BlockSpec((H_A, 1, LANES), const3),
        pl.BlockSpec((H_A, 1, LANES), const3),
        pl.BlockSpec((H_B, 1, DK_B), const3),
        pl.BlockSpec((ca, ca), lambda b, i: (0, 0)),
        pl.BlockSpec((ca, ca), lambda b, i: (0, 0)),
        pl.BlockSpec((H_B, c, c), const3),
        pl.BlockSpec((H_B, c, DK_B), const3),
        pl.BlockSpec((H_B, c, DK_B), const3),
        pl.BlockSpec((H_B, 1, DK_B), const3),
    ]
    args = [proj, cos, sin, lb, ghn, grn, tab["lv"], tab["tri"], tab["dmat"], tab["xi"],
            tab["zeta"], tab["gc"]]
    if has_state:
        in_specs += [sa_spec, sb_spec]
        args += list(states)
    aliases = {}
    if prev_states is not None:
        for k, arr in enumerate(prev_states):
            aliases[len(args)] = 1 + k
            in_specs.append(pl.BlockSpec(memory_space=pl.ANY))
            args.append(arr)
    out_specs = [
        pl.BlockSpec((N_OBLK, c, LANES), lambda b, i: (0, b * n_chunks + i, 0)), sa_spec, sb_spec]
    out_shape = [
        jax.ShapeDtypeStruct((N_OBLK, batch * n_chunks * c, LANES), BF16),
        jax.ShapeDtypeStruct((DEPTH, batch, H_A, LANES, LANES), F32),
        jax.ShapeDtypeStruct((DEPTH, batch, H_B, DK_B, DK_B), F32),
    ]
    assert not casts or batch == 1
    for w, layer in casts:
        _, r, cols = w.shape
        rb = r // n_chunks
        in_specs.append(pl.BlockSpec((1, rb, cols), lambda b, i, layer=layer: (layer, i, 0)))
        args.append(w)
        out_specs.append(pl.BlockSpec((rb, cols), lambda b, i: (i, 0)))
        out_shape.append(jax.ShapeDtypeStruct((r, cols), BF16))
    kern = functools.partial(_mixer_kernel, c=c, ca=ca, n_chunks=n_chunks, has_state=has_state,
                             n_alias=len(aliases), n_cast=len(casts))
    return pl.pallas_call(
        kern,
        grid=(batch, n_chunks),
        in_specs=in_specs,
        out_specs=out_specs,
        out_shape=out_shape,
        input_output_aliases=aliases,
        scratch_shapes=[
            pltpu.VMEM((H_A, LANES, LANES), F32),
            pltpu.VMEM((H_B, DK_B, DK_B), F32),
            pltpu.VMEM((HGRN_GROUP * (c // ca), ca, LANES), F32),
        ],
        compiler_params=_cparams(("arbitrary", "arbitrary")),
        name="mixer_state" if has_state else "mixer_prompt",
    )(*args)


def _outproj_kernel(op_ref, os_ref, x_ref, w_ref, gts_ref, gtp_ref, g2_ref, scs_ref, scp_ref,
                    shs_ref, shp_ref, xo_ref, h_ref, *, n_prompt_tiles):
    i = pl.program_id(0)
    half = TM // 2

    def run(o_ref, is_sample):
        for hf in range(2):
            o = jnp.concatenate([o_ref[cb, pl.ds(hf * half, half), :] for cb in range(N_OBLK)],
                                axis=-1)
            y = jnp.dot(o, w_ref[...], preferred_element_type=F32)
            for gl in range(half // GROUP):
                gi = hf * (half // GROUP) + gl
                rows = pl.ds(gi * GROUP, GROUP)
                xn = (x_ref[rows, :] + _mod_row(is_sample, gts_ref, gtp_ref, gi)
                      * y[gl * GROUP:(gl + 1) * GROUP, :])
                xo_ref[rows, :] = xn
                gain = g2_ref[0] * (1.0 + _mod_row(is_sample, scs_ref, scp_ref, gi))
                h_ref[rows, :] = (_rms(xn) * gain
                                  + _mod_row(is_sample, shs_ref, shp_ref, gi)).astype(BF16)

    @pl.when(i < n_prompt_tiles)
    def _():
        run(op_ref, False)

    @pl.when(i >= n_prompt_tiles)
    def _():
        run(os_ref, True)


def _outproj_call(o_p, o_s, x, mod, norm2_g, w_out_l, l, n_prompt):
    t = x.shape[0]
    npt = n_prompt // TM
    kern = functools.partial(_outproj_kernel, n_prompt_tiles=npt)
    last_p = npt - 1
    return pl.pallas_call(
        kern,
        grid=(t // TM,),
        in_specs=[
            pl.BlockSpec((N_OBLK, TM, LANES), lambda i: (0, jnp.minimum(i, last_p), 0)),
            pl.BlockSpec((N_OBLK, TM, LANES), lambda i: (0, jnp.maximum(i - npt, 0), 0)),
            pl.BlockSpec((TM, D_MODEL), lambda i: (i, 0)),
            pl.BlockSpec((D_MODEL, D_MODEL), lambda i: (0, 0)),
            *_mod_specs(l, GATE1, 1),
            pl.BlockSpec((1, 1, D_MODEL), lambda i: (l, 0, 0)),
            *_mod_specs(l, SCALE2, 1),
            *_mod_specs(l, SHIFT2, 1),
        ],
        out_specs=[
            pl.BlockSpec((TM, D_MODEL), lambda i: (i, 0)),
            pl.BlockSpec((TM, D_MODEL), lambda i: (i, 0)),
        ],
        out_shape=[
            jax.ShapeDtypeStruct((t, D_MODEL), F32),
            jax.ShapeDtypeStruct((t, D_MODEL), BF16),
        ],
        compiler_params=_cparams(("arbitrary",)),
        name="out_proj",
    )(o_p, o_s, x, w_out_l, mod, mod, norm2_g, mod, mod, mod, mod)


def _mlp_kernel(h_ref, x_ref, wu_ref, wd_ref, gts_ref, gtp_ref, fg_ref, *rest, nf, final,
                n_prompt_tiles):
    if final:
        yp_ref, ys_ref, acc_ref = rest
    else:
        xo_ref, r_ref, acc_ref = rest
    i = pl.program_id(0)
    f = pl.program_id(1)

    @pl.when(f == 0)
    def _():
        acc_ref[...] = jnp.zeros_like(acc_ref)

    u = jnp.dot(h_ref[...], wu_ref[...], preferred_element_type=F32)
    u = jnp.square(jnp.maximum(u, 0.0)).astype(BF16)
    acc_ref[...] += jnp.dot(u, wd_ref[...], preferred_element_type=F32)

    def finish(dst_ref, is_sample):
        for gi in range(TM // GROUP):
            rows = pl.ds(gi * GROUP, GROUP)
            xn = x_ref[rows, :] + _mod_row(is_sample, gts_ref, gtp_ref, gi) * acc_ref[rows, :]
            r = lax.rsqrt(jnp.mean(xn * xn, axis=-1, keepdims=True) + EPS)
            if final:
                xn = xn * r * fg_ref[...]
            else:
                r_ref[rows, :] = r
            dst_ref[rows, :] = xn

    @pl.when(f == nf - 1)
    def _():
        if final:
            @pl.when(i < n_prompt_tiles)
            def _():
                finish(yp_ref, False)

            @pl.when(i >= n_prompt_tiles)
            def _():
                finish(ys_ref, True)
        else:
            finish(xo_ref, i >= n_prompt_tiles)


def _mlp_call(h, x, mod, w_up_l, w_down_l, final_g, l, final, n_prompt, tf=1024):
    t = x.shape[0]
    nf = D_FF // tf
    npt = n_prompt // TM
    kern = functools.partial(_mlp_kernel, nf=nf, final=final, n_prompt_tiles=npt)
    if final:
        last_p = npt - 1
        out_specs = [pl.BlockSpec((TM, D_MODEL), lambda i, f: (jnp.minimum(i, last_p), 0)),
                     pl.BlockSpec((TM, D_MODEL), lambda i, f: (jnp.maximum(i - npt, 0), 0))]
        out_shape = [jax.ShapeDtypeStruct((n_prompt, D_MODEL), F32),
                     jax.ShapeDtypeStruct((t - n_prompt, D_MODEL), F32)]
    else:
        out_specs = [pl.BlockSpec((TM, D_MODEL), lambda i, f: (i, 0)),
                     pl.BlockSpec((TM, 1), lambda i, f: (i, 0))]
        out_shape = [jax.ShapeDtypeStruct((t, D_MODEL), F32), jax.ShapeDtypeStruct((t, 1), F32)]
    return pl.pallas_call(
        kern,
        grid=(t // TM, nf),
        in_specs=[
            pl.BlockSpec((TM, D_MODEL), lambda i, f: (i, 0)),
            pl.BlockSpec((TM, D_MODEL), lambda i, f: (i, 0)),
            pl.BlockSpec((D_MODEL, tf), lambda i, f: (0, f)),
            pl.BlockSpec((tf, D_MODEL), lambda i, f: (f, 0)),
            *_mod_specs(l, GATE2, 2),
            pl.BlockSpec((1, D_MODEL), lambda i, f: (0, 0)),
        ],
        out_specs=out_specs,
        out_shape=out_shape,
        scratch_shapes=[pltpu.VMEM((TM, D_MODEL), F32)],
        compiler_params=_cparams(("arbitrary", "arbitrary")),
        name="mlp_final" if final else "mlp",
    )(h, x, w_up_l, w_down_l, mod, mod, final_g)


def _rope_tables(pos):
    half = DK_B // 2
    inv_freq = 1.0 / (ROPE_BASE ** jnp.linspace(0.0, 1.0, half, dtype=F32))
    ang = pos[:, None] * inv_freq[None, :]
    return jnp.cos(ang), jnp.sin(ang)


def kernel(x_prompt, x_sample, state_hgrn, state_ret, c_prompt, c_sample, lb_logits, w_ada, b_ada,
           norm1_g, norm2_g, w_in, hgrn_norm_g, ret_norm_g, w_out, w_up, w_down, final_g):
    bp, lp, _ = x_prompt.shape
    bs, ls, _ = x_sample.shape
    assert bp == 1 and ls == GROUP and bs == N_SEQ_S and lp % TM == 0
    n_prompt = bp * lp
    n_sample = bs * ls
    c_prompt_chunk = 256
    ca_prompt = 128

    c_all = jnp.concatenate(
        [c_sample, c_prompt, jnp.zeros((MOD_ROWS - bs - bp, D_MODEL), F32)], 0)
    mod = _ada_call(c_all, w_ada, b_ada)

    p = jax.nn.softmax(lb_logits.astype(F32), axis=0)
    cs = jnp.cumsum(p, axis=0)
    lb_all = (cs - cs[0:1]).reshape(DEPTH, H_A, 1, LANES)
    log_gamma = jnp.log1p(-jnp.exp2(-5.0 - jnp.arange(H_B, dtype=F32)))
    ghn = hgrn_norm_g.reshape(DEPTH, H_A, 1, LANES)
    grn = ret_norm_g.reshape(DEPTH, H_B, 1, DK_B)
    n1 = norm1_g.reshape(DEPTH, 1, D_MODEL)
    n2 = norm2_g.reshape(DEPTH, 1, D_MODEL)
    fg = final_g.reshape(1, D_MODEL)

    cos_p, sin_p = _rope_tables(jnp.arange(lp, dtype=F32))
    cos_s, sin_s = _rope_tables(PAST_LEN + jnp.arange(ls, dtype=F32))
    tab_p = _mixer_tables(c_prompt_chunk, ca_prompt, log_gamma)
    tab_s = _mixer_tables(ls, ls, log_gamma)

    w_in_l = w_in[0].astype(BF16)
    new_p = None
    new_s = None
    for l in range(DEPTH):
        if l == 0:
            proj, x = _inproj_first_call(x_prompt.reshape(n_prompt, D_MODEL),
                                         x_sample.reshape(n_sample, D_MODEL), mod, n1, w_in_l, l)
        else:
            proj = _inproj_call(x, r, mod, n1, w_in_l, l, n_prompt)
        casts = [(w_out, l), (w_up, l), (w_down, l)] + ([(w_in, l + 1)] if l + 1 < DEPTH else [])
        res = _mixer_call(proj, cos_p, sin_p, lb_all[l], ghn[l], grn[l], tab_p, None, new_p, casts,
                          c=c_prompt_chunk, ca=ca_prompt, batch=bp,
                          n_chunks=lp // c_prompt_chunk, row0=0, l=l)
        o_p, w_out_l, w_up_l, w_down_l = res[0], res[3], res[4], res[5]
        new_p = (res[1], res[2])
        if l + 1 < DEPTH:
            w_in_l = res[6]
        res = _mixer_call(proj, cos_s, sin_s, lb_all[l], ghn[l], grn[l], tab_s,
                          (state_hgrn, state_ret), new_s, [],
                          c=ls, ca=ls, batch=bs, n_chunks=1, row0=n_prompt, l=l)
        o_s = res[0]
        new_s = (res[1], res[2])
        x, h2 = _outproj_call(o_p, o_s, x, mod, n2, w_out_l, l, n_prompt)
        res = _mlp_call(h2, x, mod, w_up_l, w_down_l, fg, l, l == DEPTH - 1, n_prompt)
        if l + 1 < DEPTH:
            x, r = res

    y_p, y_s = res
    return (y_p.reshape(bp, lp, D_MODEL), y_s.reshape(bs, ls, D_MODEL),
            new_p[0], new_p[1], new_s[0], new_s[1])
```

```python
import functools

import numpy as np
import jax
import jax.numpy as jnp
from jax import lax
from jax.experimental import pallas as pl
from jax.experimental.pallas import tpu as pltpu

F32 = jnp.float32
BF16 = jnp.bfloat16

D_MODEL = 2048
DEPTH = 4
H_A = 8
H_B = 4
DK_B = 256
N_PROJ = 8192
N_MOD = 6
D_FF = 4 * D_MODEL
EPS = 1e-6
ROPE_BASE = 10000.0
LOG2_E = 1.4426950408889634
PAST_LEN = 2048

LANES = 128
N_PBLK = N_PROJ // LANES
N_OBLK = D_MODEL // LANES
GROUP = 64
TM = 512
N_SEQ_S = TM // GROUP
MOD_ROWS = 16
HGRN_GROUP = 8

VMEM_LIMIT = 56 * 1024 * 1024

_NT = (((1,), (1,)), ((), ()))
_TN = (((0,), (0,)), ((), ()))

SHIFT1, SCALE1, GATE1, SHIFT2, SCALE2, GATE2 = range(N_MOD)


def _cparams(sem):
    return pltpu.CompilerParams(dimension_semantics=sem, vmem_limit_bytes=VMEM_LIMIT)


def _silu(x):
    hx = 0.5 * x
    return hx * jnp.tanh(hx) + hx


def _rms(x):
    return x * lax.rsqrt(jnp.mean(x * x, axis=-1, keepdims=True) + EPS)


def _mod_specs(which, ngrid):
    if ngrid == 1:
        return [pl.BlockSpec((1, N_SEQ_S, 1, D_MODEL), lambda i: (0, 0, 0, which)),
                pl.BlockSpec((1, 1, 1, D_MODEL), lambda i: (0, N_SEQ_S, 0, which))]
    return [pl.BlockSpec((1, N_SEQ_S, 1, D_MODEL), lambda i, j: (0, 0, 0, which)),
            pl.BlockSpec((1, 1, 1, D_MODEL), lambda i, j: (0, N_SEQ_S, 0, which))]


def _mod_row(is_sample, s_ref, p_ref, gi):
    if isinstance(is_sample, bool):
        return s_ref[0, gi] if is_sample else p_ref[0, 0]
    return jnp.where(is_sample, s_ref[0, gi], p_ref[0, 0])


def _ada_kernel(c_ref, w_ref, b_ref, o_ref):
    c = _silu(c_ref[...]).astype(BF16)
    w = w_ref[0].astype(BF16)
    res = jnp.dot(c, w, preferred_element_type=F32) + b_ref[0]
    for r in range(MOD_ROWS):
        o_ref[0, r] = res[r:r + 1, :]


def _ada_args(c_all, w_ada, b_ada):
    return c_all, w_ada, b_ada.reshape(DEPTH, 1, N_MOD * D_MODEL)


def _ada_call(ada_args, l, tn=1024):
    n = N_MOD * D_MODEL
    return pl.pallas_call(
        _ada_kernel,
        grid=(n // tn,),
        in_specs=[
            pl.BlockSpec((MOD_ROWS, D_MODEL), lambda j: (0, 0)),
            pl.BlockSpec((1, D_MODEL, tn), lambda j: (l, 0, j)),
            pl.BlockSpec((1, 1, tn), lambda j: (l, 0, j)),
        ],
        out_specs=pl.BlockSpec((1, MOD_ROWS, 1, tn), lambda j: (0, 0, 0, j)),
        out_shape=jax.ShapeDtypeStruct((1, MOD_ROWS, 1, n), F32),
        compiler_params=_cparams(("arbitrary",)),
        name="ada_mod",
    )(*ada_args)


def _inproj_tile(x_ref, r_ref, g_ref, scs_ref, scp_ref, shs_ref, shp_ref, w_ref, o_ref, is_sample,
                 tn):
    g = g_ref[0]
    hs = []
    for gi in range(TM // GROUP):
        rows = pl.ds(gi * GROUP, GROUP)
        gain = g * (1.0 + _mod_row(is_sample, scs_ref, scp_ref, gi))
        shift = _mod_row(is_sample, shs_ref, shp_ref, gi)
        hs.append((x_ref[rows, :] * r_ref[rows, :] * gain + shift).astype(BF16))
    h = jnp.concatenate(hs, axis=0)
    acc = jnp.dot(h, w_ref[...], preferred_element_type=F32)
    for c in range(tn // LANES):
        o_ref[c] = acc[:, c * LANES:(c + 1) * LANES].astype(BF16)


def _inproj_first_kernel(xp_ref, xs_ref, g_ref, scs_ref, scp_ref, shs_ref, shp_ref, w_ref, o_ref,
                         x_ref, r_scr, *, tn, n_prompt_tiles):
    is_sample = pl.program_id(0) >= n_prompt_tiles

    @pl.when(pl.program_id(1) == 0)
    def _():
        @pl.when(is_sample)
        def _():
            x_ref[...] = xs_ref[...]

        @pl.when(jnp.logical_not(is_sample))
        def _():
            x_ref[...] = xp_ref[...]
        x = x_ref[...]
        r_scr[...] = lax.rsqrt(jnp.mean(x * x, axis=-1, keepdims=True) + EPS)

    _inproj_tile(x_ref, r_scr, g_ref, scs_ref, scp_ref, shs_ref, shp_ref, w_ref, o_ref, is_sample,
                 tn)


def _inproj_kernel(x_ref, r_ref, g_ref, scs_ref, scp_ref, shs_ref, shp_ref, w_ref, o_ref,
                   *, tn, n_prompt_tiles):
    is_sample = pl.program_id(1) >= n_prompt_tiles
    _inproj_tile(x_ref, r_ref, g_ref, scs_ref, scp_ref, shs_ref, shp_ref, w_ref, o_ref, is_sample,
                 tn)


def _inproj_first_call(x_prompt, x_sample, mod, norm1_g, w_in_l, l, tn=2048):
    n_prompt = x_prompt.shape[0]
    t = n_prompt + x_sample.shape[0]
    npt = n_prompt // TM
    kern = functools.partial(_inproj_first_kernel, tn=tn, n_prompt_tiles=npt)
    tile = pl.BlockSpec((TM, D_MODEL), lambda i, j: (i, 0))
    return pl.pallas_call(
        kern,
        grid=(t // TM, N_PROJ // tn),
        in_specs=[
            pl.BlockSpec((TM, D_MODEL), lambda i, j: (jnp.minimum(i, npt - 1), 0)),
            pl.BlockSpec((TM, D_MODEL), lambda i, j: (jnp.maximum(i - npt, 0), 0)),
            pl.BlockSpec((1, 1, D_MODEL), lambda i, j: (l, 0, 0)),
            *_mod_specs(SCALE1, 2),
            *_mod_specs(SHIFT1, 2),
            pl.BlockSpec((D_MODEL, tn), lambda i, j: (0, j)),
        ],
        out_specs=[pl.BlockSpec((tn // LANES, TM, LANES), lambda i, j: (j, i, 0)), tile],
        out_shape=[jax.ShapeDtypeStruct((N_PBLK, t, LANES), BF16),
                   jax.ShapeDtypeStruct((t, D_MODEL), F32)],
        scratch_shapes=[pltpu.VMEM((TM, 1), F32)],
        compiler_params=_cparams(("arbitrary", "arbitrary")),
        name="in_proj_first",
    )(x_prompt, x_sample, norm1_g, mod, mod, mod, mod, w_in_l)


def _inproj_call(x, r, mod, norm1_g, w_in_l, l, n_prompt, tn=2048):
    t = x.shape[0]
    kern = functools.partial(_inproj_kernel, tn=tn, n_prompt_tiles=n_prompt // TM)
    return pl.pallas_call(
        kern,
        grid=(N_PROJ // tn, t // TM),
        in_specs=[
            pl.BlockSpec((TM, D_MODEL), lambda j, i: (i, 0)),
            pl.BlockSpec((TM, 1), lambda j, i: (i, 0)),
            pl.BlockSpec((1, 1, D_MODEL), lambda j, i: (l, 0, 0)),
            *_mod_specs(SCALE1, 2),
            *_mod_specs(SHIFT1, 2),
            pl.BlockSpec((D_MODEL, tn), lambda j, i: (0, j)),
        ],
        out_specs=pl.BlockSpec((tn // LANES, TM, LANES), lambda j, i: (j, i, 0)),
        out_shape=jax.ShapeDtypeStruct((N_PBLK, t, LANES), BF16),
        compiler_params=_cparams(("arbitrary", "arbitrary")),
        name="in_proj",
    )(x, r, norm1_g, mod, mod, mod, mod, w_in_l)


def _hgrn_ref_rows(b_scr, lvl, c):
    half = 1 << lvl
    m = 2 * half
    if half >= 8:
        blocks = [jnp.broadcast_to(b_scr[pl.ds(j * m + half - 1, 1), :], (m, LANES))
                  for j in range(c // m)]
    elif half == 4:
        blocks = [jnp.broadcast_to(b_scr[pl.ds(8 * j + 3, 1), :], (8, LANES))
                  for j in range(c // 8)]
    else:
        sub = lax.broadcasted_iota(jnp.int32, (8, LANES), 0)
        blocks = [jnp.where(sub < 4,
                            jnp.broadcast_to(b_scr[pl.ds(8 * j + 1, 1), :], (8, LANES)),
                            jnp.broadcast_to(b_scr[pl.ds(8 * j + 5, 1), :], (8, LANES)))
                  for j in range(c // 8)]
    return blocks[0] if len(blocks) == 1 else jnp.concatenate(blocks, axis=0)


def _pick_rows(row, q, k, lvl, c):
    half = 1 << lvl
    if half < 8:
        return jnp.where((row & half) != 0, q, k)
    return jnp.concatenate([(q if j & 1 else k)[j * half:(j + 1) * half]
                            for j in range(c // half)], axis=0)


def _mixer_kernel(*refs, c, ca, n_chunks, has_state, n_alias, n_cast, with_ada):
    refs = list(refs)
    (proj_ref, cos_ref, sin_ref, lb_ref, ghn_ref, grn_ref, lv_ref, tri_ref, dmat_ref, xi_ref,
     zeta_ref, gc_ref) = refs[:12]
    pos = 12
    if has_state:
        sa_in_ref, sb_in_ref = refs[pos:pos + 2]
        pos += 2
    pos += n_alias
    cast_in = refs[pos:pos + n_cast]
    pos += n_cast
    ada_in = refs[pos:pos + 3 * with_ada]
    pos += 3 * with_ada
    o_ref, sa_out_ref, sb_out_ref = refs[pos:pos + 3]
    pos += 3
    cast_out = refs[pos:pos + n_cast]
    pos += n_cast
    ada_out = refs[pos:pos + with_ada]
    pos += with_ada
    st_scr, sb_scr, b_scr = refs[pos:]

    ci = pl.program_id(1)
    n_lvl = ca.bit_length() - 1

    @pl.when(ci == 0)
    def _():
        if has_state:
            for h in range(H_A):
                st_scr[h] = sa_in_ref[0, 0, h].T
            sb_scr[...] = sb_in_ref[0, 0]
        else:
            st_scr[...] = jnp.zeros_like(st_scr)
            sb_scr[...] = jnp.zeros_like(sb_scr)

    if with_ada:
        _ada_kernel(*ada_in, *ada_out)
    for src, dst in zip(cast_in, cast_out):
        dst[...] = src[0].astype(BF16)

    row = lax.broadcasted_iota(jnp.int32, (ca, LANES), 0)

    n_sub = c // ca

    def hgrn_group(heads):
        streams = [(h, sub) for h in heads for sub in range(n_sub)]
        vals = []
        for h, sub in streams:
            rows = pl.ds(sub * ca, ca)
            lb = lb_ref[h]
            f = lb + (1.0 - lb) * jax.nn.sigmoid(proj_ref[H_A + h, rows, :].astype(F32))
            logf = jnp.log(f) * LOG2_E
            hi = logf.astype(BF16)
            lo = (logf - hi.astype(F32)).astype(BF16)
            b2 = jnp.dot(tri_ref[...], jnp.concatenate([hi, lo], axis=-1),
                         preferred_element_type=F32)
            vals.append(dict(f=f, b2=b2))

        for si, (h, sub) in enumerate(streams):
            d = vals[si]
            f, b2 = d["f"], d["b2"]
            q = _silu(proj_ref[h, pl.ds(sub * ca, ca), :].astype(F32))
            k = 1.0 - f
            b = b2[:, :LANES] + b2[:, LANES:]
            b_sub = b_scr.at[si]
            b_sub[...] = b
            z = jnp.where((row & 1) != 0, q * f, k).astype(BF16)
            p = lax.dot_general(z, z, _NT, preferred_element_type=F32)
            a = jnp.where(lv_ref[...] == 0, p.astype(BF16), jnp.zeros((), BF16))
            for lvl in range(1, n_lvl):
                e = jnp.exp2(-jnp.abs(b - _hgrn_ref_rows(b_sub, lvl, ca)))
                z = (_pick_rows(row, q, k, lvl, ca) * e).astype(BF16)
                p = lax.dot_general(z, z, _NT, preferred_element_type=F32)
                a = jnp.where(lv_ref[...] == lvl, p.astype(BF16), a)
            vb = proj_ref[2 * H_A + h, pl.ds(sub * ca, ca), :]
            b_last = b[ca - 1:ca, :]
            kd = (k * jnp.exp2(b_last - b)).astype(BF16)
            d.update(a=a, upd=lax.dot_general(vb, kd, _TN, preferred_element_type=F32),
                     decay=jnp.exp2(b_last), qi=(q * jnp.exp2(b)).astype(BF16),
                     diag=jnp.sum(q * k, axis=-1, keepdims=True))

        for gi, h in enumerate(heads):
            s_cur = st_scr[h]
            for sub in range(n_sub):
                d = vals[gi * n_sub + sub]
                vb = proj_ref[2 * H_A + h, pl.ds(sub * ca, ca), :]
                o = jnp.dot(d["a"], vb, preferred_element_type=F32)
                d["o"] = o + lax.dot_general(d["qi"], s_cur.astype(BF16), _NT,
                                             preferred_element_type=F32)
                s_cur = s_cur * d["decay"] + d["upd"]
            st_scr[h] = s_cur

        for si, (h, sub) in enumerate(streams):
            rows = pl.ds(sub * ca, ca)
            d = vals[si]
            o = d["o"] + d["diag"] * proj_ref[2 * H_A + h, rows, :].astype(F32)
            ga = proj_ref[3 * H_A + h, rows, :].astype(F32)
            o_ref[h, rows, :] = (_rms(o) * ghn_ref[h] * _silu(ga)).astype(BF16)

    for g0 in range(0, H_A, HGRN_GROUP):
        hgrn_group(list(range(g0, g0 + HGRN_GROUP)))

    cos = cos_ref[...]
    sin = sin_ref[...]

    rvals = []
    for hb in range(H_B):
        def pair(base):
            return (proj_ref[base + 2 * hb].astype(F32), proj_ref[base + 2 * hb + 1].astype(F32))

        q1, q2 = pair(4 * H_A)
        k1, k2 = pair(4 * H_A + 2 * H_B)
        vbase = 4 * H_A + 4 * H_B
        vb = jnp.concatenate([proj_ref[vbase + 2 * hb], proj_ref[vbase + 2 * hb + 1]], axis=-1)
        q = jnp.concatenate([q1 * cos - q2 * sin, q2 * cos + q1 * sin], axis=-1)
        k = jnp.concatenate([k1 * cos - k2 * sin, k2 * cos + k1 * sin], axis=-1) * (DK_B ** -0.5)
        qb = q.astype(BF16)
        s = sb_scr[hb]
        qk = lax.dot_general(qb, k.astype(BF16), _NT, preferred_element_type=F32)
        qs = jnp.dot(qb, s.astype(BF16), preferred_element_type=F32)
        kz = (k * zeta_ref[hb]).astype(BF16)
        sb_scr[hb] = gc_ref[hb] * s + lax.dot_general(kz, vb, _TN, preferred_element_type=F32)
        rvals.append((qk, qs, vb))
    for hb in range(H_B):
        qk, qs, vb = rvals[hb]
        o = jnp.dot((qk * dmat_ref[hb]).astype(BF16), vb, preferred_element_type=F32)
        o = o + qs * xi_ref[hb]
        gbase = 4 * H_A + 6 * H_B
        gate = jnp.concatenate([proj_ref[gbase + 2 * hb].astype(F32),
                                proj_ref[gbase + 2 * hb + 1].astype(F32)], axis=-1)
        out = _rms(o) * grn_ref[hb] * _silu(gate)
        o_ref[H_A + 2 * hb] = out[:, :LANES].astype(BF16)
        o_ref[H_A + 2 * hb + 1] = out[:, LANES:].astype(BF16)

    @pl.when(ci == n_chunks - 1)
    def _():
        for h in range(H_A):
            sa_out_ref[0, 0, h] = st_scr[h].T
        sb_out_ref[0, 0] = sb_scr[...]


def _mixer_tables(c, ca, log_gamma):
    t = np.arange(ca)
    x = t[:, None] ^ t[None, :]
    lv = np.where(t[:, None] > t[None, :], np.floor(np.log2(np.maximum(x, 1))), -1).astype(np.int32)
    tri = (t[:, None] >= t[None, :]).astype(np.float32)
    idx = jnp.arange(c, dtype=F32)
    lg = log_gamma[:, None, None]
    causal = jnp.asarray(np.tril(np.ones((c, c), dtype=bool)))
    dmat = jnp.where(causal[None], jnp.exp((idx[:, None] - idx[None, :])[None] * lg), 0.0)
    xi = jnp.exp((idx + 1.0)[None, :] * log_gamma[:, None])[..., None]
    zeta = jnp.exp((c - 1.0 - idx)[None, :] * log_gamma[:, None])[..., None]
    g_c = jnp.exp(c * log_gamma)[:, None, None]
    return dict(
        lv=jnp.asarray(lv, dtype=BF16), tri=jnp.asarray(tri, dtype=BF16), dmat=dmat,
        xi=jnp.broadcast_to(xi, (H_B, c, DK_B)), zeta=jnp.broadcast_to(zeta, (H_B, c, DK_B)),
        gc=jnp.broadcast_to(g_c, (H_B, 1, DK_B)))


def _mixer_call(proj, cos, sin, lb, ghn, grn, tab, states, prev_states, casts, ada, *, c, ca, batch,
                n_chunks, row0, l):
    has_state = states is not None
    blk0 = row0 // c
    const3 = lambda b, i: (0, 0, 0)
    sa_spec = pl.BlockSpec((1, 1, H_A, LANES, LANES), lambda b, i: (l, b, 0, 0, 0))
    sb_spec = pl.BlockSpec((1, 1, H_B, DK_B, DK_B), lambda b, i: (l, b, 0, 0, 0))
    in_specs = [
        pl.BlockSpec((N_PBLK, c, LANES), lambda b, i: (0, blk0 + b * n_chunks + i, 0)),
        pl.BlockSpec((c, LANES), lambda b, i: (i, 0)),
        pl.BlockSpec((c, LANES), lambda b, i: (i, 0)),
        pl.BlockSpec((H_A, 1, LANES), const3),
        pl.BlockSpec((H_A, 1, LANES), const3),
        pl.BlockSpec((H_B, 1, DK_B), const3),
        pl.BlockSpec((ca, ca), lambda b, i: (0, 0)),
        pl.BlockSpec((ca, ca), lambda b, i: (0, 0)),
        pl.BlockSpec((H_B, c, c), const3),
        pl.BlockSpec((H_B, c, DK_B), const3),
        pl.BlockSpec((H_B, c, DK_B), const3),
        pl.BlockSpec((H_B, 1, DK_B), const3),
    ]
    args = [proj, cos, sin, lb, ghn, grn, tab["lv"], tab["tri"], tab["dmat"], tab["xi"],
            tab["zeta"], tab["gc"]]
    if has_state:
        in_specs += [sa_spec, sb_spec]
        args += list(states)
    aliases = {}
    if prev_states is not None:
        for k, arr in enumerate(prev_states):
            aliases[len(args)] = 1 + k
            in_specs.append(pl.BlockSpec(memory_space=pl.ANY))
            args.append(arr)
    out_specs = [
        pl.BlockSpec((N_OBLK, c, LANES), lambda b, i: (0, b * n_chunks + i, 0)), sa_spec, sb_spec]
    out_shape = [
        jax.ShapeDtypeStruct((N_OBLK, batch * n_chunks * c, LANES), BF16),
        jax.ShapeDtypeStruct((DEPTH, batch, H_A, LANES, LANES), F32),
        jax.ShapeDtypeStruct((DEPTH, batch, H_B, DK_B, DK_B), F32),
    ]
    assert not casts or batch == 1
    for w, layer in casts:
        _, r, cols = w.shape
        rb = r // n_chunks
        in_specs.append(pl.BlockSpec((1, rb, cols), lambda b, i, layer=layer: (layer, i, 0)))
        args.append(w)
        out_specs.append(pl.BlockSpec((rb, cols), lambda b, i: (i, 0)))
        out_shape.append(jax.ShapeDtypeStruct((r, cols), BF16))
    if ada is not None:
        assert batch == 1
        ada_args, layer = ada
        n = N_MOD * D_MODEL
        tn = n // n_chunks
        in_specs += [pl.BlockSpec((MOD_ROWS, D_MODEL), lambda b, i: (0, 0)),
                     pl.BlockSpec((1, D_MODEL, tn), lambda b, i: (layer, 0, i)),
                     pl.BlockSpec((1, 1, tn), lambda b, i: (layer, 0, i))]
        args += list(ada_args)
        out_specs.append(pl.BlockSpec((1, MOD_ROWS, 1, tn), lambda b, i: (0, 0, 0, i)))
        out_shape.append(jax.ShapeDtypeStruct((1, MOD_ROWS, 1, n), F32))
    kern = functools.partial(_mixer_kernel, c=c, ca=ca, n_chunks=n_chunks, has_state=has_state,
                             n_alias=len(aliases), n_cast=len(casts), with_ada=int(ada is not None))
    return pl.pallas_call(
        kern,
        grid=(batch, n_chunks),
        in_specs=in_specs,
        out_specs=out_specs,
        out_shape=out_shape,
        input_output_aliases=aliases,
        scratch_shapes=[
            pltpu.VMEM((H_A, LANES, LANES), F32),
            pltpu.VMEM((H_B, DK_B, DK_B), F32),
            pltpu.VMEM((HGRN_GROUP * (c // ca), ca, LANES), F32),
        ],
        compiler_params=_cparams(("arbitrary", "arbitrary")),
        name="mixer_state" if has_state else "mixer_prompt",
    )(*args)


def _outproj_kernel(op_ref, os_ref, x_ref, w_ref, gts_ref, gtp_ref, g2_ref, scs_ref, scp_ref,
                    shs_ref, shp_ref, xo_ref, h_ref, *, n_prompt_tiles):
    i = pl.program_id(0)
    half = TM // 2

    def run(o_ref, is_sample):
        for hf in range(2):
            o = jnp.concatenate([o_ref[cb, pl.ds(hf * half, half), :] for cb in range(N_OBLK)],
                                axis=-1)
            y = jnp.dot(o, w_ref[...], preferred_element_type=F32)
            for gl in range(half // GROUP):
                gi = hf * (half // GROUP) + gl
                rows = pl.ds(gi * GROUP, GROUP)
                xn = (x_ref[rows, :] + _mod_row(is_sample, gts_ref, gtp_ref, gi)
                      * y[gl * GROUP:(gl + 1) * GROUP, :])
                xo_ref[rows, :] = xn
                gain = g2_ref[0] * (1.0 + _mod_row(is_sample, scs_ref, scp_ref, gi))
                h_ref[rows, :] = (_rms(xn) * gain
                                  + _mod_row(is_sample, shs_ref, shp_ref, gi)).astype(BF16)

    @pl.when(i < n_prompt_tiles)
    def _():
        run(op_ref, False)

    @pl.when(i >= n_prompt_tiles)
    def _():
        run(os_ref, True)


def _outproj_call(o_p, o_s, x, mod, norm2_g, w_out_l, l, n_prompt):
    t = x.shape[0]
    npt = n_prompt // TM
    kern = functools.partial(_outproj_kernel, n_prompt_tiles=npt)
    last_p = npt - 1
    return pl.pallas_call(
        kern,
        grid=(t // TM,),
        in_specs=[
            pl.BlockSpec((N_OBLK, TM, LANES), lambda i: (0, jnp.minimum(i, last_p), 0)),
            pl.BlockSpec((N_OBLK, TM, LANES), lambda i: (0, jnp.maximum(i - npt, 0), 0)),
            pl.BlockSpec((TM, D_MODEL), lambda i: (i, 0)),
            pl.BlockSpec((D_MODEL, D_MODEL), lambda i: (0, 0)),
            *_mod_specs(GATE1, 1),
            pl.BlockSpec((1, 1, D_MODEL), lambda i: (l, 0, 0)),
            *_mod_specs(SCALE2, 1),
            *_mod_specs(SHIFT2, 1),
        ],
        out_specs=[
            pl.BlockSpec((TM, D_MODEL), lambda i: (i, 0)),
            pl.BlockSpec((TM, D_MODEL), lambda i: (i, 0)),
        ],
        out_shape=[
            jax.ShapeDtypeStruct((t, D_MODEL), F32),
            jax.ShapeDtypeStruct((t, D_MODEL), BF16),
        ],
        compiler_params=_cparams(("arbitrary",)),
        name="out_proj",
    )(o_p, o_s, x, w_out_l, mod, mod, norm2_g, mod, mod, mod, mod)


def _mlp_kernel(h_ref, x_ref, wu_ref, wd_ref, gts_ref, gtp_ref, fg_ref, *rest, nf, final,
                n_prompt_tiles):
    if final:
        yp_ref, ys_ref, acc_ref = rest
    else:
        xo_ref, r_ref, acc_ref = rest
    i = pl.program_id(0)
    f = pl.program_id(1)

    @pl.when(f == 0)
    def _():
        acc_ref[...] = jnp.zeros_like(acc_ref)

    u = jnp.dot(h_ref[...], wu_ref[...], preferred_element_type=F32)
    u = jnp.square(jnp.maximum(u, 0.0)).astype(BF16)
    acc_ref[...] += jnp.dot(u, wd_ref[...], preferred_element_type=F32)

    def finish(dst_ref, is_sample):
        for gi in range(TM // GROUP):
            rows = pl.ds(gi * GROUP, GROUP)
            xn = x_ref[rows, :] + _mod_row(is_sample, gts_ref, gtp_ref, gi) * acc_ref[rows, :]
            r = lax.rsqrt(jnp.mean(xn * xn, axis=-1, keepdims=True) + EPS)
            if final:
                xn = xn * r * fg_ref[...]
            else:
                r_ref[rows, :] = r
            dst_ref[rows, :] = xn

    @pl.when(f == nf - 1)
    def _():
        if final:
            @pl.when(i < n_prompt_tiles)
            def _():
                finish(yp_ref, False)

            @pl.when(i >= n_prompt_tiles)
            def _():
                finish(ys_ref, True)
        else:
            finish(xo_ref, i >= n_prompt_tiles)


def _mlp_call(h, x, mod, w_up_l, w_down_l, final_g, l, final, n_prompt, tf=1024):
    t = x.shape[0]
    nf = D_FF // tf
    npt = n_prompt // TM
    kern = functools.partial(_mlp_kernel, nf=nf, final=final, n_prompt_tiles=npt)
    if final:
        last_p = npt - 1
        out_specs = [pl.BlockSpec((TM, D_MODEL), lambda i, f: (jnp.minimum(i, last_p), 0)),
                     pl.BlockSpec((TM, D_MODEL), lambda i, f: (jnp.maximum(i - npt, 0), 0))]
        out_shape = [jax.ShapeDtypeStruct((n_prompt, D_MODEL), F32),
                     jax.ShapeDtypeStruct((t - n_prompt, D_MODEL), F32)]
    else:
        out_specs = [pl.BlockSpec((TM, D_MODEL), lambda i, f: (i, 0)),
                     pl.BlockSpec((TM, 1), lambda i, f: (i, 0))]
        out_shape = [jax.ShapeDtypeStruct((t, D_MODEL), F32), jax.ShapeDtypeStruct((t, 1), F32)]
    return pl.pallas_call(
        kern,
        grid=(t // TM, nf),
        in_specs=[
            pl.BlockSpec((TM, D_MODEL), lambda i, f: (i, 0)),
            pl.BlockSpec((TM, D_MODEL), lambda i, f: (i, 0)),
            pl.BlockSpec((D_MODEL, tf), lambda i, f: (0, f)),
            pl.BlockSpec((tf, D_MODEL), lambda i, f: (f, 0)),
            *_mod_specs(GATE2, 2),
            pl.BlockSpec((1, D_MODEL), lambda i, f: (0, 0)),
        ],
        out_specs=out_specs,
        out_shape=out_shape,
        scratch_shapes=[pltpu.VMEM((TM, D_MODEL), F32)],
        compiler_params=_cparams(("arbitrary", "arbitrary")),
        name="mlp_final" if final else "mlp",
    )(h, x, w_up_l, w_down_l, mod, mod, final_g)


def _rope_tables(pos):
    half = DK_B // 2
    inv_freq = 1.0 / (ROPE_BASE ** jnp.linspace(0.0, 1.0, half, dtype=F32))
    ang = pos[:, None] * inv_freq[None, :]
    return jnp.cos(ang), jnp.sin(ang)


def kernel(x_prompt, x_sample, state_hgrn, state_ret, c_prompt, c_sample, lb_logits, w_ada, b_ada,
           norm1_g, norm2_g, w_in, hgrn_norm_g, ret_norm_g, w_out, w_up, w_down, final_g):
    bp, lp, _ = x_prompt.shape
    bs, ls, _ = x_sample.shape
    assert bp == 1 and ls == GROUP and bs == N_SEQ_S and lp % TM == 0
    n_prompt = bp * lp
    n_sample = bs * ls
    c_prompt_chunk = 256
    ca_prompt = 128

    c_all = jnp.concatenate(
        [c_sample, c_prompt, jnp.zeros((MOD_ROWS - bs - bp, D_MODEL), F32)], 0)
    ada_args = _ada_args(c_all, w_ada, b_ada)
    mod = _ada_call(ada_args, 0)

    p = jax.nn.softmax(lb_logits.astype(F32), axis=0)
    cs = jnp.cumsum(p, axis=0)
    lb_all = (cs - cs[0:1]).reshape(DEPTH, H_A, 1, LANES)
    log_gamma = jnp.log1p(-jnp.exp2(-5.0 - jnp.arange(H_B, dtype=F32)))
    ghn = hgrn_norm_g.reshape(DEPTH, H_A, 1, LANES)
    grn = ret_norm_g.reshape(DEPTH, H_B, 1, DK_B)
    n1 = norm1_g.reshape(DEPTH, 1, D_MODEL)
    n2 = norm2_g.reshape(DEPTH, 1, D_MODEL)
    fg = final_g.reshape(1, D_MODEL)

    cos_p, sin_p = _rope_tables(jnp.arange(lp, dtype=F32))
    cos_s, sin_s = _rope_tables(PAST_LEN + jnp.arange(ls, dtype=F32))
    tab_p = _mixer_tables(c_prompt_chunk, ca_prompt, log_gamma)
    tab_s = _mixer_tables(ls, ls, log_gamma)

    w_in_l = w_in[0].astype(BF16)
    new_p = (jnp.zeros((DEPTH, bp, H_A, LANES, LANES), F32), jnp.zeros((DEPTH, bp, H_B, DK_B, DK_B), F32))
    new_s = (jnp.zeros((DEPTH, bs, H_A, LANES, LANES), F32), jnp.zeros((DEPTH, bs, H_B, DK_B, DK_B), F32))
    for l in range(DEPTH):
        if l == 0:
            proj, x = _inproj_first_call(x_prompt.reshape(n_prompt, D_MODEL),
                                         x_sample.reshape(n_sample, D_MODEL), mod, n1, w_in_l, l)
        else:
            proj = _inproj_call(x, r, mod, n1, w_in_l, l, n_prompt)
        more = l + 1 < DEPTH
        casts = [(w_out, l), (w_up, l), (w_down, l)] + ([(w_in, l + 1)] if more else [])
        res = _mixer_call(proj, cos_p, sin_p, lb_all[l], ghn[l], grn[l], tab_p, None, new_p, casts,
                          (ada_args, l + 1) if more else None,
                          c=c_prompt_chunk, ca=ca_prompt, batch=bp,
                          n_chunks=lp // c_prompt_chunk, row0=0, l=l)
        o_p, w_out_l, w_up_l, w_down_l = res[0], res[3], res[4], res[5]
        new_p = (res[1], res[2])
        if more:
            w_in_l, mod_next = res[6], res[7]
        res = _mixer_call(proj, cos_s, sin_s, lb_all[l], ghn[l], grn[l], tab_s,
                          (state_hgrn, state_ret), new_s, [], None,
                          c=ls, ca=ls, batch=bs, n_chunks=1, row0=n_prompt, l=l)
        o_s = res[0]
        new_s = (res[1], res[2])
        x, h2 = _outproj_call(o_p, o_s, x, mod, n2, w_out_l, l, n_prompt)
        res = _mlp_call(h2, x, mod, w_up_l, w_down_l, fg, l, not more, n_prompt)
        if more:
            x, r = res
            mod = mod_next

    y_p, y_s = res
    return (y_p.reshape(bp, lp, D_MODEL), y_s.reshape(bs, ls, D_MODEL),
            new_p[0], new_p[1], new_s[0], new_s[1])
```

```python
import functools

import numpy as np
import jax
import jax.numpy as jnp
from jax import lax
from jax.experimental import pallas as pl
from jax.experimental.pallas import tpu as pltpu

F32 = jnp.float32
BF16 = jnp.bfloat16

D_MODEL = 2048
DEPTH = 4
H_A = 8
H_B = 4
DK_B = 256
N_PROJ = 8192
N_MOD = 6
D_FF = 4 * D_MODEL
EPS = 1e-6
ROPE_BASE = 10000.0
LOG2_E = 1.4426950408889634
PAST_LEN = 2048

LANES = 128
N_PBLK = N_PROJ // LANES
N_OBLK = D_MODEL // LANES
GROUP = 64
TM = 512
N_SEQ_S = TM // GROUP
MOD_ROWS = 16
HGRN_GROUP = 8

VMEM_LIMIT = 56 * 1024 * 1024

_NT = (((1,), (1,)), ((), ()))
_TN = (((0,), (0,)), ((), ()))

SHIFT1, SCALE1, GATE1, SHIFT2, SCALE2, GATE2 = range(N_MOD)


def _cparams(sem):
    return pltpu.CompilerParams(dimension_semantics=sem, vmem_limit_bytes=VMEM_LIMIT)


def _silu(x):
    hx = 0.5 * x
    return hx * jnp.tanh(hx) + hx


def _rms(x):
    return x * lax.rsqrt(jnp.mean(x * x, axis=-1, keepdims=True) + EPS)


def _mod_specs(which, ngrid):
    if ngrid == 1:
        return [pl.BlockSpec((1, N_SEQ_S, 1, D_MODEL), lambda i: (0, 0, 0, which)),
                pl.BlockSpec((1, 1, 1, D_MODEL), lambda i: (0, N_SEQ_S, 0, which))]
    return [pl.BlockSpec((1, N_SEQ_S, 1, D_MODEL), lambda i, j: (0, 0, 0, which)),
            pl.BlockSpec((1, 1, 1, D_MODEL), lambda i, j: (0, N_SEQ_S, 0, which))]


def _mod_row(is_sample, s_ref, p_ref, gi):
    if isinstance(is_sample, bool):
        return s_ref[0, gi] if is_sample else p_ref[0, 0]
    return jnp.where(is_sample, s_ref[0, gi], p_ref[0, 0])


def _ada_kernel(c_ref, w_ref, b_ref, o_ref):
    c = _silu(c_ref[...]).astype(BF16)
    w = w_ref[0].astype(BF16)
    res = jnp.dot(c, w, preferred_element_type=F32) + b_ref[0]
    for r in range(MOD_ROWS):
        o_ref[0, r] = res[r:r + 1, :]


def _ada_args(c_all, w_ada, b_ada):
    return c_all, w_ada, b_ada.reshape(DEPTH, 1, N_MOD * D_MODEL)


def _ada_call(ada_args, l, tn=1024):
    n = N_MOD * D_MODEL
    return pl.pallas_call(
        _ada_kernel,
        grid=(n // tn,),
        in_specs=[
            pl.BlockSpec((MOD_ROWS, D_MODEL), lambda j: (0, 0)),
            pl.BlockSpec((1, D_MODEL, tn), lambda j: (l, 0, j)),
            pl.BlockSpec((1, 1, tn), lambda j: (l, 0, j)),
        ],
        out_specs=pl.BlockSpec((1, MOD_ROWS, 1, tn), lambda j: (0, 0, 0, j)),
        out_shape=jax.ShapeDtypeStruct((1, MOD_ROWS, 1, n), F32),
        compiler_params=_cparams(("arbitrary",)),
        name="ada_mod",
    )(*ada_args)


def _inproj_tile(x_ref, r_ref, g_ref, scs_ref, scp_ref, shs_ref, shp_ref, w_ref, o_ref, is_sample,
                 tn):
    g = g_ref[0]
    hs = []
    for gi in range(TM // GROUP):
        rows = pl.ds(gi * GROUP, GROUP)
        gain = g * (1.0 + _mod_row(is_sample, scs_ref, scp_ref, gi))
        shift = _mod_row(is_sample, shs_ref, shp_ref, gi)
        hs.append((x_ref[rows, :] * r_ref[rows, :] * gain + shift).astype(BF16))
    h = jnp.concatenate(hs, axis=0)
    acc = jnp.dot(h, w_ref[...], preferred_element_type=F32)
    for c in range(tn // LANES):
        o_ref[c] = acc[:, c * LANES:(c + 1) * LANES].astype(BF16)


def _inproj_first_kernel(xp_ref, xs_ref, g_ref, scs_ref, scp_ref, shs_ref, shp_ref, w_ref, o_ref,
                         x_ref, r_scr, *, tn, n_prompt_tiles):
    is_sample = pl.program_id(0) >= n_prompt_tiles

    @pl.when(pl.program_id(1) == 0)
    def _():
        @pl.when(is_sample)
        def _():
            x_ref[...] = xs_ref[...]

        @pl.when(jnp.logical_not(is_sample))
        def _():
            x_ref[...] = xp_ref[...]
        x = x_ref[...]
        r_scr[...] = lax.rsqrt(jnp.mean(x * x, axis=-1, keepdims=True) + EPS)

    _inproj_tile(x_ref, r_scr, g_ref, scs_ref, scp_ref, shs_ref, shp_ref, w_ref, o_ref, is_sample,
                 tn)


def _inproj_kernel(x_ref, r_ref, g_ref, scs_ref, scp_ref, shs_ref, shp_ref, w_ref, *rest,
                   tn, n_prompt_tiles, n_cast):
    cast_in, o_ref, cast_out = rest[:n_cast], rest[n_cast], rest[n_cast + 1:]
    for src, dst in zip(cast_in, cast_out):
        dst[...] = src[0].astype(BF16)
    is_sample = pl.program_id(1) >= n_prompt_tiles
    _inproj_tile(x_ref, r_ref, g_ref, scs_ref, scp_ref, shs_ref, shp_ref, w_ref, o_ref, is_sample,
                 tn)


def _inproj_first_call(x_prompt, x_sample, mod, norm1_g, w_in_l, l, tn=2048):
    n_prompt = x_prompt.shape[0]
    t = n_prompt + x_sample.shape[0]
    npt = n_prompt // TM
    kern = functools.partial(_inproj_first_kernel, tn=tn, n_prompt_tiles=npt)
    tile = pl.BlockSpec((TM, D_MODEL), lambda i, j: (i, 0))
    return pl.pallas_call(
        kern,
        grid=(t // TM, N_PROJ // tn),
        in_specs=[
            pl.BlockSpec((TM, D_MODEL), lambda i, j: (jnp.minimum(i, npt - 1), 0)),
            pl.BlockSpec((TM, D_MODEL), lambda i, j: (jnp.maximum(i - npt, 0), 0)),
            pl.BlockSpec((1, 1, D_MODEL), lambda i, j: (l, 0, 0)),
            *_mod_specs(SCALE1, 2),
            *_mod_specs(SHIFT1, 2),
            pl.BlockSpec((D_MODEL, tn), lambda i, j: (0, j)),
        ],
        out_specs=[pl.BlockSpec((tn // LANES, TM, LANES), lambda i, j: (j, i, 0)), tile],
        out_shape=[jax.ShapeDtypeStruct((N_PBLK, t, LANES), BF16),
                   jax.ShapeDtypeStruct((t, D_MODEL), F32)],
        scratch_shapes=[pltpu.VMEM((TM, 1), F32)],
        compiler_params=_cparams(("arbitrary", "arbitrary")),
        name="in_proj_first",
    )(x_prompt, x_sample, norm1_g, mod, mod, mod, mod, w_in_l)


def _inproj_call(x, r, mod, norm1_g, w_in_l, casts, l, n_prompt, tn=2048):
    t = x.shape[0]
    npt = n_prompt // TM
    n_slab = (N_PROJ // tn) * npt
    kern = functools.partial(_inproj_kernel, tn=tn, n_prompt_tiles=npt, n_cast=len(casts))
    slab = lambda j, i: j * npt + jnp.minimum(i, npt - 1)
    in_specs = [
        pl.BlockSpec((TM, D_MODEL), lambda j, i: (i, 0)),
        pl.BlockSpec((TM, 1), lambda j, i: (i, 0)),
        pl.BlockSpec((1, 1, D_MODEL), lambda j, i: (l, 0, 0)),
        *_mod_specs(SCALE1, 2),
        *_mod_specs(SHIFT1, 2),
        pl.BlockSpec((D_MODEL, tn), lambda j, i: (0, j)),
    ]
    args = [x, r, norm1_g, mod, mod, mod, mod, w_in_l]
    out_specs = [pl.BlockSpec((tn // LANES, TM, LANES), lambda j, i: (j, i, 0))]
    out_shape = [jax.ShapeDtypeStruct((N_PBLK, t, LANES), BF16)]
    for w, layer in casts:
        _, rows, cols = w.shape
        rb = rows // n_slab
        in_specs.append(pl.BlockSpec((1, rb, cols),
                                     lambda j, i, layer=layer: (layer, slab(j, i), 0)))
        args.append(w)
        out_specs.append(pl.BlockSpec((rb, cols), lambda j, i: (slab(j, i), 0)))
        out_shape.append(jax.ShapeDtypeStruct((rows, cols), BF16))
    return pl.pallas_call(
        kern,
        grid=(N_PROJ // tn, t // TM),
        in_specs=in_specs,
        out_specs=out_specs,
        out_shape=out_shape,
        compiler_params=_cparams(("arbitrary", "arbitrary")),
        name="in_proj",
    )(*args)


def _hgrn_ref_rows(b_scr, lvl, c):
    half = 1 << lvl
    m = 2 * half
    if half >= 8:
        blocks = [jnp.broadcast_to(b_scr[pl.ds(j * m + half - 1, 1), :], (m, LANES))
                  for j in range(c // m)]
    elif half == 4:
        blocks = [jnp.broadcast_to(b_scr[pl.ds(8 * j + 3, 1), :], (8, LANES))
                  for j in range(c // 8)]
    else:
        sub = lax.broadcasted_iota(jnp.int32, (8, LANES), 0)
        blocks = [jnp.where(sub < 4,
                            jnp.broadcast_to(b_scr[pl.ds(8 * j + 1, 1), :], (8, LANES)),
                            jnp.broadcast_to(b_scr[pl.ds(8 * j + 5, 1), :], (8, LANES)))
                  for j in range(c // 8)]
    return blocks[0] if len(blocks) == 1 else jnp.concatenate(blocks, axis=0)


def _pick_rows(row, q, k, lvl, c):
    half = 1 << lvl
    if half < 8:
        return jnp.where((row & half) != 0, q, k)
    return jnp.concatenate([(q if j & 1 else k)[j * half:(j + 1) * half]
                            for j in range(c // half)], axis=0)


def _mixer_kernel(*refs, c, ca, n_chunks, has_state, n_alias, n_cast, with_ada):
    refs = list(refs)
    (proj_ref, cos_ref, sin_ref, lb_ref, ghn_ref, grn_ref, lv_ref, tri_ref, dmat_ref, xi_ref,
     zeta_ref, gc_ref) = refs[:12]
    pos = 12
    if has_state:
        sa_in_ref, sb_in_ref = refs[pos:pos + 2]
        pos += 2
    pos += n_alias
    cast_in = refs[pos:pos + n_cast]
    pos += n_cast
    ada_in = refs[pos:pos + 3 * with_ada]
    pos += 3 * with_ada
    o_ref, sa_out_ref, sb_out_ref = refs[pos:pos + 3]
    pos += 3
    cast_out = refs[pos:pos + n_cast]
    pos += n_cast
    ada_out = refs[pos:pos + with_ada]
    pos += with_ada
    st_scr, sb_scr, b_scr = refs[pos:]

    ci = pl.program_id(1)
    n_lvl = ca.bit_length() - 1

    @pl.when(ci == 0)
    def _():
        if has_state:
            for h in range(H_A):
                st_scr[h] = sa_in_ref[0, 0, h].T
            sb_scr[...] = sb_in_ref[0, 0]
        else:
            st_scr[...] = jnp.zeros_like(st_scr)
            sb_scr[...] = jnp.zeros_like(sb_scr)

    if with_ada:
        _ada_kernel(*ada_in, *ada_out)
    for src, dst in zip(cast_in, cast_out):
        dst[...] = src[0].astype(BF16)

    row = lax.broadcasted_iota(jnp.int32, (ca, LANES), 0)

    n_sub = c // ca

    def hgrn_group(heads):
        streams = [(h, sub) for h in heads for sub in range(n_sub)]
        vals = []
        for h, sub in streams:
            rows = pl.ds(sub * ca, ca)
            lb = lb_ref[h]
            f = lb + (1.0 - lb) * jax.nn.sigmoid(proj_ref[H_A + h, rows, :].astype(F32))
            logf = jnp.log(f) * LOG2_E
            hi = logf.astype(BF16)
            lo = (logf - hi.astype(F32)).astype(BF16)
            b2 = jnp.dot(tri_ref[...], jnp.concatenate([hi, lo], axis=-1),
                         preferred_element_type=F32)
            vals.append(dict(f=f, b2=b2))

        for si, (h, sub) in enumerate(streams):
            d = vals[si]
            f, b2 = d["f"], d["b2"]
            q = _silu(proj_ref[h, pl.ds(sub * ca, ca), :].astype(F32))
            k = 1.0 - f
            b = b2[:, :LANES] + b2[:, LANES:]
            b_sub = b_scr.at[si]
            b_sub[...] = b
            z = jnp.where((row & 1) != 0, q * f, k).astype(BF16)
            p = lax.dot_general(z, z, _NT, preferred_element_type=F32)
            a = jnp.where(lv_ref[...] == 0, p.astype(BF16), jnp.zeros((), BF16))
            for lvl in range(1, n_lvl):
                e = jnp.exp2(-jnp.abs(b - _hgrn_ref_rows(b_sub, lvl, ca)))
                z = (_pick_rows(row, q, k, lvl, ca) * e).astype(BF16)
                p = lax.dot_general(z, z, _NT, preferred_element_type=F32)
                a = jnp.where(lv_ref[...] == lvl, p.astype(BF16), a)
            vb = proj_ref[2 * H_A + h, pl.ds(sub * ca, ca), :]
            b_last = b[ca - 1:ca, :]
            kd = (k * jnp.exp2(b_last - b)).astype(BF16)
            d.update(a=a, upd=lax.dot_general(vb, kd, _TN, preferred_element_type=F32),
                     decay=jnp.exp2(b_last), qi=(q * jnp.exp2(b)).astype(BF16),
                     diag=jnp.sum(q * k, axis=-1, keepdims=True))

        for gi, h in enumerate(heads):
            s_cur = st_scr[h]
            for sub in range(n_sub):
                d = vals[gi * n_sub + sub]
                vb = proj_ref[2 * H_A + h, pl.ds(sub * ca, ca), :]
                o = jnp.dot(d["a"], vb, preferred_element_type=F32)
                d["o"] = o + lax.dot_general(d["qi"], s_cur.astype(BF16), _NT,
                                             preferred_element_type=F32)
                s_cur = s_cur * d["decay"] + d["upd"]
            st_scr[h] = s_cur

        for si, (h, sub) in enumerate(streams):
            rows = pl.ds(sub * ca, ca)
            d = vals[si]
            o = d["o"] + d["diag"] * proj_ref[2 * H_A + h, rows, :].astype(F32)
            ga = proj_ref[3 * H_A + h, rows, :].astype(F32)
            o_ref[h, rows, :] = (_rms(o) * ghn_ref[h] * _silu(ga)).astype(BF16)

    for g0 in range(0, H_A, HGRN_GROUP):
        hgrn_group(list(range(g0, g0 + HGRN_GROUP)))

    cos = cos_ref[...]
    sin = sin_ref[...]

    rvals = []
    for hb in range(H_B):
        def pair(base):
            return (proj_ref[base + 2 * hb].astype(F32), proj_ref[base + 2 * hb + 1].astype(F32))

        q1, q2 = pair(4 * H_A)
        k1, k2 = pair(4 * H_A + 2 * H_B)
        vbase = 4 * H_A + 4 * H_B
        vb = jnp.concatenate([proj_ref[vbase + 2 * hb], proj_ref[vbase + 2 * hb + 1]], axis=-1)
        q = jnp.concatenate([q1 * cos - q2 * sin, q2 * cos + q1 * sin], axis=-1)
        k = jnp.concatenate([k1 * cos - k2 * sin, k2 * cos + k1 * sin], axis=-1) * (DK_B ** -0.5)
        qb = q.astype(BF16)
        s = sb_scr[hb]
        qk = lax.dot_general(qb, k.astype(BF16), _NT, preferred_element_type=F32)
        qs = jnp.dot(qb, s.astype(BF16), preferred_element_type=F32)
        kz = (k * zeta_ref[hb]).astype(BF16)
        sb_scr[hb] = gc_ref[hb] * s + lax.dot_general(kz, vb, _TN, preferred_element_type=F32)
        rvals.append((qk, qs, vb))
    for hb in range(H_B):
        qk, qs, vb = rvals[hb]
        o = jnp.dot((qk * dmat_ref[hb]).astype(BF16), vb, preferred_element_type=F32)
        o = o + qs * xi_ref[hb]
        gbase = 4 * H_A + 6 * H_B
        gate = jnp.concatenate([proj_ref[gbase + 2 * hb].astype(F32),
                                proj_ref[gbase + 2 * hb + 1].astype(F32)], axis=-1)
        out = _rms(o) * grn_ref[hb] * _silu(gate)
        o_ref[H_A + 2 * hb] = out[:, :LANES].astype(BF16)
        o_ref[H_A + 2 * hb + 1] = out[:, LANES:].astype(BF16)

    @pl.when(ci == n_chunks - 1)
    def _():
        for h in range(H_A):
            sa_out_ref[0, 0, h] = st_scr[h].T
        sb_out_ref[0, 0] = sb_scr[...]


def _mixer_tables(c, ca, log_gamma):
    t = np.arange(ca)
    x = t[:, None] ^ t[None, :]
    lv = np.where(t[:, None] > t[None, :], np.floor(np.log2(np.maximum(x, 1))), -1).astype(np.int32)
    tri = (t[:, None] >= t[None, :]).astype(np.float32)
    idx = jnp.arange(c, dtype=F32)
    lg = log_gamma[:, None, None]
    causal = jnp.asarray(np.tril(np.ones((c, c), dtype=bool)))
    dmat = jnp.where(causal[None], jnp.exp((idx[:, None] - idx[None, :])[None] * lg), 0.0)
    xi = jnp.exp((idx + 1.0)[None, :] * log_gamma[:, None])[..., None]
    zeta = jnp.exp((c - 1.0 - idx)[None, :] * log_gamma[:, None])[..., None]
    g_c = jnp.exp(c * log_gamma)[:, None, None]
    return dict(
        lv=jnp.asarray(lv, dtype=BF16), tri=jnp.asarray(tri, dtype=BF16), dmat=dmat,
        xi=jnp.broadcast_to(xi, (H_B, c, DK_B)), zeta=jnp.broadcast_to(zeta, (H_B, c, DK_B)),
        gc=jnp.broadcast_to(g_c, (H_B, 1, DK_B)))


def _mixer_call(proj, cos, sin, lb, ghn, grn, tab, states, prev_states, casts, ada, *, c, ca, batch,
                n_chunks, row0, l):
    has_state = states is not None
    blk0 = row0 // c
    const3 = lambda b, i: (0, 0, 0)
    sa_spec = pl.BlockSpec((1, 1, H_A, LANES, LANES), lambda b, i: (l, b, 0, 0, 0))
    sb_spec = pl.BlockSpec((1, 1, H_B, DK_B, DK_B), lambda b, i: (l, b, 0, 0, 0))
    in_specs = [
        pl.BlockSpec((N_PBLK, c, LANES), lambda b, i: (0, blk0 + b * n_chunks + i, 0)),
        pl.BlockSpec((c, LANES), lambda b, i: (i, 0)),
        pl.BlockSpec((c, LANES), lambda b, i: (i, 0)),
        pl.BlockSpec((H_A, 1, LANES), const3),
        pl.BlockSpec((H_A, 1, LANES), const3),
        pl.BlockSpec((H_B, 1, DK_B), const3),
        pl.BlockSpec((ca, ca), lambda b, i: (0, 0)),
        pl.BlockSpec((ca, ca), lambda b, i: (0, 0)),
        pl.BlockSpec((H_B, c, c), const3),
        pl.BlockSpec((H_B, c, DK_B), const3),
        pl.BlockSpec((H_B, c, DK_B), const3),
        pl.BlockSpec((H_B, 1, DK_B), const3),
    ]
    args = [proj, cos, sin, lb, ghn, grn, tab["lv"], tab["tri"], tab["dmat"], tab["xi"],
            tab["zeta"], tab["gc"]]
    if has_state:
        in_specs += [sa_spec, sb_spec]
        args += list(states)
    aliases = {}
    if prev_states is not None:
        for k, arr in enumerate(prev_states):
            aliases[len(args)] = 1 + k
            in_specs.append(pl.BlockSpec(memory_space=pl.ANY))
            args.append(arr)
    out_specs = [
        pl.BlockSpec((N_OBLK, c, LANES), lambda b, i: (0, b * n_chunks + i, 0)), sa_spec, sb_spec]
    out_shape = [
        jax.ShapeDtypeStruct((N_OBLK, batch * n_chunks * c, LANES), BF16),
        jax.ShapeDtypeStruct((DEPTH, batch, H_A, LANES, LANES), F32),
        jax.ShapeDtypeStruct((DEPTH, batch, H_B, DK_B, DK_B), F32),
    ]
    assert not casts or batch == 1
    for w, layer in casts:
        _, r, cols = w.shape
        rb = r // n_chunks
        in_specs.append(pl.BlockSpec((1, rb, cols), lambda b, i, layer=layer: (layer, i, 0)))
        args.append(w)
        out_specs.append(pl.BlockSpec((rb, cols), lambda b, i: (i, 0)))
        out_shape.append(jax.ShapeDtypeStruct((r, cols), BF16))
    if ada is not None:
        assert batch == 1
        ada_args, layer = ada
        n = N_MOD * D_MODEL
        tn = n // n_chunks
        in_specs += [pl.BlockSpec((MOD_ROWS, D_MODEL), lambda b, i: (0, 0)),
                     pl.BlockSpec((1, D_MODEL, tn), lambda b, i: (layer, 0, i)),
                     pl.BlockSpec((1, 1, tn), lambda b, i: (layer, 0, i))]
        args += list(ada_args)
        out_specs.append(pl.BlockSpec((1, MOD_ROWS, 1, tn), lambda b, i: (0, 0, 0, i)))
        out_shape.append(jax.ShapeDtypeStruct((1, MOD_ROWS, 1, n), F32))
    kern = functools.partial(_mixer_kernel, c=c, ca=ca, n_chunks=n_chunks, has_state=has_state,
                             n_alias=len(aliases), n_cast=len(casts), with_ada=int(ada is not None))
    return pl.pallas_call(
        kern,
        grid=(batch, n_chunks),
        in_specs=in_specs,
        out_specs=out_specs,
        out_shape=out_shape,
        input_output_aliases=aliases,
        scratch_shapes=[
            pltpu.VMEM((H_A, LANES, LANES), F32),
            pltpu.VMEM((H_B, DK_B, DK_B), F32),
            pltpu.VMEM((HGRN_GROUP * (c // ca), ca, LANES), F32),
        ],
        compiler_params=_cparams(("arbitrary", "arbitrary")),
        name="mixer_state" if has_state else "mixer_prompt",
    )(*args)


def _outproj_kernel(op_ref, os_ref, x_ref, w_ref, gts_ref, gtp_ref, g2_ref, scs_ref, scp_ref,
                    shs_ref, shp_ref, xo_ref, h_ref, *, n_prompt_tiles):
    i = pl.program_id(0)
    half = TM // 2

    def run(o_ref, is_sample):
        for hf in range(2):
            o = jnp.concatenate([o_ref[cb, pl.ds(hf * half, half), :] for cb in range(N_OBLK)],
                                axis=-1)
            y = jnp.dot(o, w_ref[...], preferred_element_type=F32)
            for gl in range(half // GROUP):
                gi = hf * (half // GROUP) + gl
                rows = pl.ds(gi * GROUP, GROUP)
                xn = (x_ref[rows, :] + _mod_row(is_sample, gts_ref, gtp_ref, gi)
                      * y[gl * GROUP:(gl + 1) * GROUP, :])
                xo_ref[rows, :] = xn
                gain = g2_ref[0] * (1.0 + _mod_row(is_sample, scs_ref, scp_ref, gi))
                h_ref[rows, :] = (_rms(xn) * gain
                                  + _mod_row(is_sample, shs_ref, shp_ref, gi)).astype(BF16)

    @pl.when(i < n_prompt_tiles)
    def _():
        run(op_ref, False)

    @pl.when(i >= n_prompt_tiles)
    def _():
        run(os_ref, True)


def _outproj_call(o_p, o_s, x, mod, norm2_g, w_out_l, l, n_prompt):
    t = x.shape[0]
    npt = n_prompt // TM
    kern = functools.partial(_outproj_kernel, n_prompt_tiles=npt)
    last_p = npt - 1
    return pl.pallas_call(
        kern,
        grid=(t // TM,),
        in_specs=[
            pl.BlockSpec((N_OBLK, TM, LANES), lambda i: (0, jnp.minimum(i, last_p), 0)),
            pl.BlockSpec((N_OBLK, TM, LANES), lambda i: (0, jnp.maximum(i - npt, 0), 0)),
            pl.BlockSpec((TM, D_MODEL), lambda i: (i, 0)),
            pl.BlockSpec((D_MODEL, D_MODEL), lambda i: (0, 0)),
            *_mod_specs(GATE1, 1),
            pl.BlockSpec((1, 1, D_MODEL), lambda i: (l, 0, 0)),
            *_mod_specs(SCALE2, 1),
            *_mod_specs(SHIFT2, 1),
        ],
        out_specs=[
            pl.BlockSpec((TM, D_MODEL), lambda i: (i, 0)),
            pl.BlockSpec((TM, D_MODEL), lambda i: (i, 0)),
        ],
        out_shape=[
            jax.ShapeDtypeStruct((t, D_MODEL), F32),
            jax.ShapeDtypeStruct((t, D_MODEL), BF16),
        ],
        compiler_params=_cparams(("arbitrary",)),
        name="out_proj",
    )(o_p, o_s, x, w_out_l, mod, mod, norm2_g, mod, mod, mod, mod)


def _mlp_kernel(h_ref, x_ref, wu_ref, wd_ref, gts_ref, gtp_ref, fg_ref, *rest, nf, final,
                n_prompt_tiles):
    if final:
        yp_ref, ys_ref, acc_ref = rest
    else:
        xo_ref, r_ref, acc_ref = rest
    i = pl.program_id(0)
    f = pl.program_id(1)

    @pl.when(f == 0)
    def _():
        acc_ref[...] = jnp.zeros_like(acc_ref)

    u = jnp.dot(h_ref[...], wu_ref[...], preferred_element_type=F32)
    u = jnp.square(jnp.maximum(u, 0.0)).astype(BF16)
    acc_ref[...] += jnp.dot(u, wd_ref[...], preferred_element_type=F32)

    def finish(dst_ref, is_sample):
        for gi in range(TM // GROUP):
            rows = pl.ds(gi * GROUP, GROUP)
            xn = x_ref[rows, :] + _mod_row(is_sample, gts_ref, gtp_ref, gi) * acc_ref[rows, :]
            r = lax.rsqrt(jnp.mean(xn * xn, axis=-1, keepdims=True) + EPS)
            if final:
                xn = xn * r * fg_ref[...]
            else:
                r_ref[rows, :] = r
            dst_ref[rows, :] = xn

    @pl.when(f == nf - 1)
    def _():
        if final:
            @pl.when(i < n_prompt_tiles)
            def _():
                finish(yp_ref, False)

            @pl.when(i >= n_prompt_tiles)
            def _():
                finish(ys_ref, True)
        else:
            finish(xo_ref, i >= n_prompt_tiles)


def _mlp_call(h, x, mod, w_up_l, w_down_l, final_g, l, final, n_prompt, tf=1024):
    t = x.shape[0]
    nf = D_FF // tf
    npt = n_prompt // TM
    kern = functools.partial(_mlp_kernel, nf=nf, final=final, n_prompt_tiles=npt)
    if final:
        last_p = npt - 1
        out_specs = [pl.BlockSpec((TM, D_MODEL), lambda i, f: (jnp.minimum(i, last_p), 0)),
                     pl.BlockSpec((TM, D_MODEL), lambda i, f: (jnp.maximum(i - npt, 0), 0))]
        out_shape = [jax.ShapeDtypeStruct((n_prompt, D_MODEL), F32),
                     jax.ShapeDtypeStruct((t - n_prompt, D_MODEL), F32)]
    else:
        out_specs = [pl.BlockSpec((TM, D_MODEL), lambda i, f: (i, 0)),
                     pl.BlockSpec((TM, 1), lambda i, f: (i, 0))]
        out_shape = [jax.ShapeDtypeStruct((t, D_MODEL), F32), jax.ShapeDtypeStruct((t, 1), F32)]
    return pl.pallas_call(
        kern,
        grid=(t // TM, nf),
        in_specs=[
            pl.BlockSpec((TM, D_MODEL), lambda i, f: (i, 0)),
            pl.BlockSpec((TM, D_MODEL), lambda i, f: (i, 0)),
            pl.BlockSpec((D_MODEL, tf), lambda i, f: (0, f)),
            pl.BlockSpec((tf, D_MODEL), lambda i, f: (f, 0)),
            *_mod_specs(GATE2, 2),
            pl.BlockSpec((1, D_MODEL), lambda i, f: (0, 0)),
        ],
        out_specs=out_specs,
        out_shape=out_shape,
        scratch_shapes=[pltpu.VMEM((TM, D_MODEL), F32)],
        compiler_params=_cparams(("arbitrary", "arbitrary")),
        name="mlp_final" if final else "mlp",
    )(h, x, w_up_l, w_down_l, mod, mod, final_g)


def _rope_tables(pos):
    half = DK_B // 2
    inv_freq = 1.0 / (ROPE_BASE ** jnp.linspace(0.0, 1.0, half, dtype=F32))
    ang = pos[:, None] * inv_freq[None, :]
    return jnp.cos(ang), jnp.sin(ang)


def kernel(x_prompt, x_sample, state_hgrn, state_ret, c_prompt, c_sample, lb_logits, w_ada, b_ada,
           norm1_g, norm2_g, w_in, hgrn_norm_g, ret_norm_g, w_out, w_up, w_down, final_g):
    bp, lp, _ = x_prompt.shape
    bs, ls, _ = x_sample.shape
    assert bp == 1 and ls == GROUP and bs == N_SEQ_S and lp % TM == 0
    n_prompt = bp * lp
    n_sample = bs * ls
    c_prompt_chunk = 256
    ca_prompt = 128

    c_all = jnp.concatenate(
        [c_sample, c_prompt, jnp.zeros((MOD_ROWS - bs - bp, D_MODEL), F32)], 0)
    ada_args = _ada_args(c_all, w_ada, b_ada)
    mod = _ada_call(ada_args, 0)

    p = jax.nn.softmax(lb_logits.astype(F32), axis=0)
    cs = jnp.cumsum(p, axis=0)
    lb_all = (cs - cs[0:1]).reshape(DEPTH, H_A, 1, LANES)
    log_gamma = jnp.log1p(-jnp.exp2(-5.0 - jnp.arange(H_B, dtype=F32)))
    ghn = hgrn_norm_g.reshape(DEPTH, H_A, 1, LANES)
    grn = ret_norm_g.reshape(DEPTH, H_B, 1, DK_B)
    n1 = norm1_g.reshape(DEPTH, 1, D_MODEL)
    n2 = norm2_g.reshape(DEPTH, 1, D_MODEL)
    fg = final_g.reshape(1, D_MODEL)

    cos_p, sin_p = _rope_tables(jnp.arange(lp, dtype=F32))
    cos_s, sin_s = _rope_tables(PAST_LEN + jnp.arange(ls, dtype=F32))
    tab_p = _mixer_tables(c_prompt_chunk, ca_prompt, log_gamma)
    tab_s = _mixer_tables(ls, ls, log_gamma)

    w_in_l = w_in[0].astype(BF16)
    new_p = (jnp.zeros((DEPTH, bp, H_A, LANES, LANES), F32), jnp.zeros((DEPTH, bp, H_B, DK_B, DK_B), F32))
    new_s = (jnp.zeros((DEPTH, bs, H_A, LANES, LANES), F32), jnp.zeros((DEPTH, bs, H_B, DK_B, DK_B), F32))
    for l in range(DEPTH):
        if l == 0:
            proj, x = _inproj_first_call(x_prompt.reshape(n_prompt, D_MODEL),
                                         x_sample.reshape(n_sample, D_MODEL), mod, n1, w_in_l, l)
        else:
            proj, w_up_l, w_down_l = _inproj_call(x, r, mod, n1, w_in_l,
                                                  [(w_up, l), (w_down, l)], l, n_prompt)
        more = l + 1 < DEPTH
        casts = [(w_out, l)] + ([(w_up, l), (w_down, l)] if l == 0 else [])
        casts += [(w_in, l + 1)] if more else []
        res = _mixer_call(proj, cos_p, sin_p, lb_all[l], ghn[l], grn[l], tab_p, None, new_p, casts,
                          (ada_args, l + 1) if more else None,
                          c=c_prompt_chunk, ca=ca_prompt, batch=bp,
                          n_chunks=lp // c_prompt_chunk, row0=0, l=l)
        o_p, w_out_l = res[0], res[3]
        new_p = (res[1], res[2])
        if l == 0:
            w_up_l, w_down_l = res[4], res[5]
        if more:
            w_in_l, mod_next = res[3 + len(casts) - 1], res[3 + len(casts)]
        res = _mixer_call(proj, cos_s, sin_s, lb_all[l], ghn[l], grn[l], tab_s,
                          (state_hgrn, state_ret), new_s, [], None,
                          c=ls, ca=ls, batch=bs, n_chunks=1, row0=n_prompt, l=l)
        o_s = res[0]
        new_s = (res[1], res[2])
        x, h2 = _outproj_call(o_p, o_s, x, mod, n2, w_out_l, l, n_prompt)
        res = _mlp_call(h2, x, mod, w_up_l, w_down_l, fg, l, not more, n_prompt)
        if more:
            x, r = res
            mod = mod_next

    y_p, y_s = res
    return (y_p.reshape(bp, lp, D_MODEL), y_s.reshape(bs, ls, D_MODEL),
            new_p[0], new_p[1], new_s[0], new_s[1])
```

```python
import functools

import numpy as np
import jax
import jax.numpy as jnp
from jax import lax
from jax.experimental import pallas as pl
from jax.experimental.pallas import tpu as pltpu

F32 = jnp.float32
BF16 = jnp.bfloat16

D_MODEL = 2048
DEPTH = 4
H_A = 8
H_B = 4
DK_B = 256
N_PROJ = 8192
N_MOD = 6
D_FF = 4 * D_MODEL
EPS = 1e-6
ROPE_BASE = 10000.0
LOG2_E = 1.4426950408889634
PAST_LEN = 2048

LANES = 128
N_PBLK = N_PROJ // LANES
N_OBLK = D_MODEL // LANES
GROUP = 64
TM = 512
N_SEQ_S = TM // GROUP
MOD_ROWS = 16
HGRN_GROUP = 8

VMEM_LIMIT = 56 * 1024 * 1024

_NT = (((1,), (1,)), ((), ()))
_TN = (((0,), (0,)), ((), ()))

SHIFT1, SCALE1, GATE1, SHIFT2, SCALE2, GATE2 = range(N_MOD)


def _cparams(sem):
    return pltpu.CompilerParams(dimension_semantics=sem, vmem_limit_bytes=VMEM_LIMIT)


def _silu(x):
    hx = 0.5 * x
    return hx * jnp.tanh(hx) + hx


def _rms(x):
    return x * lax.rsqrt(jnp.mean(x * x, axis=-1, keepdims=True) + EPS)


def _mod_specs(which, ngrid):
    if ngrid == 1:
        return [pl.BlockSpec((1, N_SEQ_S, 1, D_MODEL), lambda i: (0, 0, 0, which)),
                pl.BlockSpec((1, 1, 1, D_MODEL), lambda i: (0, N_SEQ_S, 0, which))]
    return [pl.BlockSpec((1, N_SEQ_S, 1, D_MODEL), lambda i, j: (0, 0, 0, which)),
            pl.BlockSpec((1, 1, 1, D_MODEL), lambda i, j: (0, N_SEQ_S, 0, which))]


def _mod_row(is_sample, s_ref, p_ref, gi):
    if isinstance(is_sample, bool):
        return s_ref[0, gi] if is_sample else p_ref[0, 0]
    return jnp.where(is_sample, s_ref[0, gi], p_ref[0, 0])


def _ada_kernel(c_ref, w_ref, b_ref, o_ref):
    c = _silu(c_ref[...]).astype(BF16)
    w = w_ref[0].astype(BF16)
    res = jnp.dot(c, w, preferred_element_type=F32) + b_ref[0]
    for r in range(MOD_ROWS):
        o_ref[0, r] = res[r:r + 1, :]


def _ada_args(c_all, w_ada, b_ada):
    return c_all, w_ada, b_ada.reshape(DEPTH, 1, N_MOD * D_MODEL)


def _ada_call(ada_args, l, tn=1024):
    n = N_MOD * D_MODEL
    return pl.pallas_call(
        _ada_kernel,
        grid=(n // tn,),
        in_specs=[
            pl.BlockSpec((MOD_ROWS, D_MODEL), lambda j: (0, 0)),
            pl.BlockSpec((1, D_MODEL, tn), lambda j: (l, 0, j)),
            pl.BlockSpec((1, 1, tn), lambda j: (l, 0, j)),
        ],
        out_specs=pl.BlockSpec((1, MOD_ROWS, 1, tn), lambda j: (0, 0, 0, j)),
        out_shape=jax.ShapeDtypeStruct((1, MOD_ROWS, 1, n), F32),
        compiler_params=_cparams(("arbitrary",)),
        name="ada_mod",
    )(*ada_args)


def _inproj_tile(x_ref, r_ref, g_ref, scs_ref, scp_ref, shs_ref, shp_ref, w_ref, o_ref, is_sample,
                 tn):
    g = g_ref[0]
    hs = []
    for gi in range(TM // GROUP):
        rows = pl.ds(gi * GROUP, GROUP)
        gain = g * (1.0 + _mod_row(is_sample, scs_ref, scp_ref, gi))
        shift = _mod_row(is_sample, shs_ref, shp_ref, gi)
        hs.append((x_ref[rows, :] * r_ref[rows, :] * gain + shift).astype(BF16))
    h = jnp.concatenate(hs, axis=0)
    acc = jnp.dot(h, w_ref[...], preferred_element_type=F32)
    for c in range(tn // LANES):
        o_ref[c] = acc[:, c * LANES:(c + 1) * LANES].astype(BF16)


def _inproj_first_kernel(xp_ref, xs_ref, g_ref, scs_ref, scp_ref, shs_ref, shp_ref, w_ref, o_ref,
                         x_ref, r_scr, *, tn, n_prompt_tiles):
    is_sample = pl.program_id(0) >= n_prompt_tiles

    @pl.when(pl.program_id(1) == 0)
    def _():
        @pl.when(is_sample)
        def _():
            x_ref[...] = xs_ref[...]

        @pl.when(jnp.logical_not(is_sample))
        def _():
            x_ref[...] = xp_ref[...]
        x = x_ref[...]
        r_scr[...] = lax.rsqrt(jnp.mean(x * x, axis=-1, keepdims=True) + EPS)

    _inproj_tile(x_ref, r_scr, g_ref, scs_ref, scp_ref, shs_ref, shp_ref, w_ref, o_ref, is_sample,
                 tn)


def _inproj_kernel(x_ref, r_ref, g_ref, scs_ref, scp_ref, shs_ref, shp_ref, w_ref, *rest,
                   tn, n_prompt_tiles, n_cast):
    cast_in, o_ref, cast_out = rest[:n_cast], rest[n_cast], rest[n_cast + 1:]
    for src, dst in zip(cast_in, cast_out):
        dst[...] = src[0].astype(BF16)
    is_sample = pl.program_id(1) >= n_prompt_tiles
    _inproj_tile(x_ref, r_ref, g_ref, scs_ref, scp_ref, shs_ref, shp_ref, w_ref, o_ref, is_sample,
                 tn)


def _inproj_first_call(x_prompt, x_sample, mod, norm1_g, w_in_l, l, tn=2048):
    n_prompt = x_prompt.shape[0]
    t = n_prompt + x_sample.shape[0]
    npt = n_prompt // TM
    kern = functools.partial(_inproj_first_kernel, tn=tn, n_prompt_tiles=npt)
    tile = pl.BlockSpec((TM, D_MODEL), lambda i, j: (i, 0))
    return pl.pallas_call(
        kern,
        grid=(t // TM, N_PROJ // tn),
        in_specs=[
            pl.BlockSpec((TM, D_MODEL), lambda i, j: (jnp.minimum(i, npt - 1), 0)),
            pl.BlockSpec((TM, D_MODEL), lambda i, j: (jnp.maximum(i - npt, 0), 0)),
            pl.BlockSpec((1, 1, D_MODEL), lambda i, j: (l, 0, 0)),
            *_mod_specs(SCALE1, 2),
            *_mod_specs(SHIFT1, 2),
            pl.BlockSpec((D_MODEL, tn), lambda i, j: (0, j)),
        ],
        out_specs=[pl.BlockSpec((tn // LANES, TM, LANES), lambda i, j: (j, i, 0)), tile],
        out_shape=[jax.ShapeDtypeStruct((N_PBLK, t, LANES), BF16),
                   jax.ShapeDtypeStruct((t, D_MODEL), F32)],
        scratch_shapes=[pltpu.VMEM((TM, 1), F32)],
        compiler_params=_cparams(("arbitrary", "arbitrary")),
        name="in_proj_first",
    )(x_prompt, x_sample, norm1_g, mod, mod, mod, mod, w_in_l)


def _inproj_call(x, r, mod, norm1_g, w_in_l, casts, l, n_prompt, tn=2048):
    t = x.shape[0]
    npt = n_prompt // TM
    n_slab = (N_PROJ // tn) * npt
    kern = functools.partial(_inproj_kernel, tn=tn, n_prompt_tiles=npt, n_cast=len(casts))
    slab = lambda j, i: j * npt + jnp.minimum(i, npt - 1)
    in_specs = [
        pl.BlockSpec((TM, D_MODEL), lambda j, i: (i, 0)),
        pl.BlockSpec((TM, 1), lambda j, i: (i, 0)),
        pl.BlockSpec((1, 1, D_MODEL), lambda j, i: (l, 0, 0)),
        *_mod_specs(SCALE1, 2),
        *_mod_specs(SHIFT1, 2),
        pl.BlockSpec((D_MODEL, tn), lambda j, i: (0, j)),
    ]
    args = [x, r, norm1_g, mod, mod, mod, mod, w_in_l]
    out_specs = [pl.BlockSpec((tn // LANES, TM, LANES), lambda j, i: (j, i, 0))]
    out_shape = [jax.ShapeDtypeStruct((N_PBLK, t, LANES), BF16)]
    for w, layer in casts:
        _, rows, cols = w.shape
        rb = rows // n_slab
        in_specs.append(pl.BlockSpec((1, rb, cols),
                                     lambda j, i, layer=layer: (layer, slab(j, i), 0)))
        args.append(w)
        out_specs.append(pl.BlockSpec((rb, cols), lambda j, i: (slab(j, i), 0)))
        out_shape.append(jax.ShapeDtypeStruct((rows, cols), BF16))
    return pl.pallas_call(
        kern,
        grid=(N_PROJ // tn, t // TM),
        in_specs=in_specs,
        out_specs=out_specs,
        out_shape=out_shape,
        compiler_params=_cparams(("arbitrary", "arbitrary")),
        name="in_proj",
    )(*args)


def _hgrn_ref_rows(b_scr, lvl, c):
    half = 1 << lvl
    m = 2 * half
    if half >= 8:
        blocks = [jnp.broadcast_to(b_scr[pl.ds(j * m + half - 1, 1), :], (m, LANES))
                  for j in range(c // m)]
    elif half == 4:
        blocks = [jnp.broadcast_to(b_scr[pl.ds(8 * j + 3, 1), :], (8, LANES))
                  for j in range(c // 8)]
    else:
        sub = lax.broadcasted_iota(jnp.int32, (8, LANES), 0)
        blocks = [jnp.where(sub < 4,
                            jnp.broadcast_to(b_scr[pl.ds(8 * j + 1, 1), :], (8, LANES)),
                            jnp.broadcast_to(b_scr[pl.ds(8 * j + 5, 1), :], (8, LANES)))
                  for j in range(c // 8)]
    return blocks[0] if len(blocks) == 1 else jnp.concatenate(blocks, axis=0)


def _pick_rows(row, q, k, lvl, c):
    half = 1 << lvl
    if half < 8:
        return jnp.where((row & half) != 0, q, k)
    return jnp.concatenate([(q if j & 1 else k)[j * half:(j + 1) * half]
                            for j in range(c // half)], axis=0)


def _mixer_kernel(*refs, c, ca, n_chunks, has_state, n_alias, n_cast, with_ada):
    refs = list(refs)
    (proj_ref, cos_ref, sin_ref, lb_ref, ghn_ref, grn_ref, lv_ref, tri_ref, dmat_ref, xi_ref,
     zeta_ref, gc_ref) = refs[:12]
    pos = 12
    if has_state:
        sa_in_ref, sb_in_ref = refs[pos:pos + 2]
        pos += 2
    pos += n_alias
    cast_in = refs[pos:pos + n_cast]
    pos += n_cast
    ada_in = refs[pos:pos + 3 * with_ada]
    pos += 3 * with_ada
    o_ref, sa_out_ref, sb_out_ref = refs[pos:pos + 3]
    pos += 3
    cast_out = refs[pos:pos + n_cast]
    pos += n_cast
    ada_out = refs[pos:pos + with_ada]
    pos += with_ada
    st_scr, sb_scr, b_scr = refs[pos:]

    ci = pl.program_id(1)
    n_lvl = ca.bit_length() - 1

    @pl.when(ci == 0)
    def _():
        if has_state:
            for h in range(H_A):
                st_scr[h] = sa_in_ref[0, 0, h].T
            sb_scr[...] = sb_in_ref[0, 0]
        else:
            st_scr[...] = jnp.zeros_like(st_scr)
            sb_scr[...] = jnp.zeros_like(sb_scr)

    if with_ada:
        _ada_kernel(*ada_in, *ada_out)
    for src, dst in zip(cast_in, cast_out):
        dst[...] = src[0].astype(BF16)

    row = lax.broadcasted_iota(jnp.int32, (ca, LANES), 0)

    n_sub = c // ca

    def hgrn_group(heads):
        streams = [(h, sub) for h in heads for sub in range(n_sub)]
        vals = []
        for h, sub in streams:
            rows = pl.ds(sub * ca, ca)
            lb = lb_ref[h]
            f = lb + (1.0 - lb) * jax.nn.sigmoid(proj_ref[H_A + h, rows, :].astype(F32))
            logf = jnp.log(f) * LOG2_E
            hi = logf.astype(BF16)
            lo = (logf - hi.astype(F32)).astype(BF16)
            b2 = jnp.dot(tri_ref[...], jnp.concatenate([hi, lo], axis=-1),
                         preferred_element_type=F32)
            vals.append(dict(f=f, b2=b2))

        for si, (h, sub) in enumerate(streams):
            d = vals[si]
            f, b2 = d["f"], d["b2"]
            q = _silu(proj_ref[h, pl.ds(sub * ca, ca), :].astype(F32))
            k = 1.0 - f
            b = b2[:, :LANES] + b2[:, LANES:]
            b_sub = b_scr.at[si]
            b_sub[...] = b
            z = jnp.where((row & 1) != 0, q * f, k).astype(BF16)
            p = lax.dot_general(z, z, _NT, preferred_element_type=F32)
            a = jnp.where(lv_ref[...] == 0, p.astype(BF16), jnp.zeros((), BF16))
            for lvl in range(1, n_lvl):
                e = jnp.exp2(-jnp.abs(b - _hgrn_ref_rows(b_sub, lvl, ca)))
                z = (_pick_rows(row, q, k, lvl, ca) * e).astype(BF16)
                p = lax.dot_general(z, z, _NT, preferred_element_type=F32)
                a = jnp.where(lv_ref[...] == lvl, p.astype(BF16), a)
            vb = proj_ref[2 * H_A + h, pl.ds(sub * ca, ca), :]
            b_last = b[ca - 1:ca, :]
            kd = (k * jnp.exp2(b_last - b)).astype(BF16)
            d.update(a=a, upd=lax.dot_general(vb, kd, _TN, preferred_element_type=F32),
                     decay=jnp.exp2(b_last), qi=(q * jnp.exp2(b)).astype(BF16),
                     diag=jnp.sum(q * k, axis=-1, keepdims=True))

        for gi, h in enumerate(heads):
            s_cur = st_scr[h]
            for sub in range(n_sub):
                d = vals[gi * n_sub + sub]
                vb = proj_ref[2 * H_A + h, pl.ds(sub * ca, ca), :]
                o = jnp.dot(d["a"], vb, preferred_element_type=F32)
                d["o"] = o + lax.dot_general(d["qi"], s_cur.astype(BF16), _NT,
                                             preferred_element_type=F32)
                s_cur = s_cur * d["decay"] + d["upd"]
            st_scr[h] = s_cur

        for si, (h, sub) in enumerate(streams):
            rows = pl.ds(sub * ca, ca)
            d = vals[si]
            o = d["o"] + d["diag"] * proj_ref[2 * H_A + h, rows, :].astype(F32)
            ga = proj_ref[3 * H_A + h, rows, :].astype(F32)
            o_ref[h, rows, :] = (_rms(o) * ghn_ref[h] * _silu(ga)).astype(BF16)

    for g0 in range(0, H_A, HGRN_GROUP):
        hgrn_group(list(range(g0, g0 + HGRN_GROUP)))

    cos = cos_ref[...]
    sin = sin_ref[...]

    rvals = []
    for hb in range(H_B):
        def pair(base):
            return (proj_ref[base + 2 * hb].astype(F32), proj_ref[base + 2 * hb + 1].astype(F32))

        q1, q2 = pair(4 * H_A)
        k1, k2 = pair(4 * H_A + 2 * H_B)
        vbase = 4 * H_A + 4 * H_B
        vb = jnp.concatenate([proj_ref[vbase + 2 * hb], proj_ref[vbase + 2 * hb + 1]], axis=-1)
        q = jnp.concatenate([q1 * cos - q2 * sin, q2 * cos + q1 * sin], axis=-1)
        k = jnp.concatenate([k1 * cos - k2 * sin, k2 * cos + k1 * sin], axis=-1) * (DK_B ** -0.5)
        qb = q.astype(BF16)
        s = sb_scr[hb]
        qk = lax.dot_general(qb, k.astype(BF16), _NT, preferred_element_type=F32)
        qs = jnp.dot(qb, s.astype(BF16), preferred_element_type=F32)
        kz = (k * zeta_ref[hb]).astype(BF16)
        sb_scr[hb] = gc_ref[hb] * s + lax.dot_general(kz, vb, _TN, preferred_element_type=F32)
        rvals.append((qk, qs, vb))
    for hb in range(H_B):
        qk, qs, vb = rvals[hb]
        o = jnp.dot((qk * dmat_ref[hb]).astype(BF16), vb, preferred_element_type=F32)
        o = o + qs * xi_ref[hb]
        gbase = 4 * H_A + 6 * H_B
        gate = jnp.concatenate([proj_ref[gbase + 2 * hb].astype(F32),
                                proj_ref[gbase + 2 * hb + 1].astype(F32)], axis=-1)
        out = _rms(o) * grn_ref[hb] * _silu(gate)
        o_ref[H_A + 2 * hb] = out[:, :LANES].astype(BF16)
        o_ref[H_A + 2 * hb + 1] = out[:, LANES:].astype(BF16)

    @pl.when(ci == n_chunks - 1)
    def _():
        for h in range(H_A):
            sa_out_ref[0, 0, h] = st_scr[h].T
        sb_out_ref[0, 0] = sb_scr[...]


def _mixer_tables(c, ca, log_gamma):
    t = np.arange(ca)
    x = t[:, None] ^ t[None, :]
    lv = np.where(t[:, None] > t[None, :], np.floor(np.log2(np.maximum(x, 1))), -1).astype(np.int32)
    tri = (t[:, None] >= t[None, :]).astype(np.float32)
    idx = jnp.arange(c, dtype=F32)
    lg = log_gamma[:, None, None]
    causal = jnp.asarray(np.tril(np.ones((c, c), dtype=bool)))
    dmat = jnp.where(causal[None], jnp.exp((idx[:, None] - idx[None, :])[None] * lg), 0.0)
    xi = jnp.exp((idx + 1.0)[None, :] * log_gamma[:, None])[..., None]
    zeta = jnp.exp((c - 1.0 - idx)[None, :] * log_gamma[:, None])[..., None]
    g_c = jnp.exp(c * log_gamma)[:, None, None]
    return dict(
        lv=jnp.asarray(lv, dtype=BF16), tri=jnp.asarray(tri, dtype=BF16), dmat=dmat,
        xi=jnp.broadcast_to(xi, (H_B, c, DK_B)), zeta=jnp.broadcast_to(zeta, (H_B, c, DK_B)),
        gc=jnp.broadcast_to(g_c, (H_B, 1, DK_B)))


def _mixer_call(proj, cos, sin, lb, ghn, grn, tab, states, prev_states, casts, ada, *, c, ca, batch,
                n_chunks, row0, l):
    has_state = states is not None
    blk0 = row0 // c
    const3 = lambda b, i: (0, 0, 0)
    sa_spec = pl.BlockSpec((1, 1, H_A, LANES, LANES), lambda b, i: (l, b, 0, 0, 0))
    sb_spec = pl.BlockSpec((1, 1, H_B, DK_B, DK_B), lambda b, i: (l, b, 0, 0, 0))
    in_specs = [
        pl.BlockSpec((N_PBLK, c, LANES), lambda b, i: (0, blk0 + b * n_chunks + i, 0)),
        pl.BlockSpec((c, LANES), lambda b, i: (i, 0)),
        pl.BlockSpec((c, LANES), lambda b, i: (i, 0)),
        pl.BlockSpec((H_A, 1, LANES), const3),
        pl.BlockSpec((H_A, 1, LANES), const3),
        pl.BlockSpec((H_B, 1, DK_B), const3),
        pl.BlockSpec((ca, ca), lambda b, i: (0, 0)),
        pl.BlockSpec((ca, ca), lambda b, i: (0, 0)),
        pl.BlockSpec((H_B, c, c), const3),
        pl.BlockSpec((H_B, c, DK_B), const3),
        pl.BlockSpec((H_B, c, DK_B), const3),
        pl.BlockSpec((H_B, 1, DK_B), const3),
    ]
    args = [proj, cos, sin, lb, ghn, grn, tab["lv"], tab["tri"], tab["dmat"], tab["xi"],
            tab["zeta"], tab["gc"]]
    if has_state:
        in_specs += [sa_spec, sb_spec]
        args += list(states)
    aliases = {}
    if prev_states is not None:
        for k, arr in enumerate(prev_states):
            aliases[len(args)] = 1 + k
            in_specs.append(pl.BlockSpec(memory_space=pl.ANY))
            args.append(arr)
    out_specs = [
        pl.BlockSpec((N_OBLK, c, LANES), lambda b, i: (0, b * n_chunks + i, 0)), sa_spec, sb_spec]
    out_shape = [
        jax.ShapeDtypeStruct((N_OBLK, batch * n_chunks * c, LANES), BF16),
        jax.ShapeDtypeStruct((DEPTH, batch, H_A, LANES, LANES), F32),
        jax.ShapeDtypeStruct((DEPTH, batch, H_B, DK_B, DK_B), F32),
    ]
    assert not casts or batch == 1
    for w, layer in casts:
        _, r, cols = w.shape
        rb = r // n_chunks
        in_specs.append(pl.BlockSpec((1, rb, cols), lambda b, i, layer=layer: (layer, i, 0)))
        args.append(w)
        out_specs.append(pl.BlockSpec((rb, cols), lambda b, i: (i, 0)))
        out_shape.append(jax.ShapeDtypeStruct((r, cols), BF16))
    if ada is not None:
        assert batch == 1
        ada_args, layer = ada
        n = N_MOD * D_MODEL
        tn = n // n_chunks
        in_specs += [pl.BlockSpec((MOD_ROWS, D_MODEL), lambda b, i: (0, 0)),
                     pl.BlockSpec((1, D_MODEL, tn), lambda b, i: (layer, 0, i)),
                     pl.BlockSpec((1, 1, tn), lambda b, i: (layer, 0, i))]
        args += list(ada_args)
        out_specs.append(pl.BlockSpec((1, MOD_ROWS, 1, tn), lambda b, i: (0, 0, 0, i)))
        out_shape.append(jax.ShapeDtypeStruct((1, MOD_ROWS, 1, n), F32))
    kern = functools.partial(_mixer_kernel, c=c, ca=ca, n_chunks=n_chunks, has_state=has_state,
                             n_alias=len(aliases), n_cast=len(casts), with_ada=int(ada is not None))
    return pl.pallas_call(
        kern,
        grid=(batch, n_chunks),
        in_specs=in_specs,
        out_specs=out_specs,
        out_shape=out_shape,
        input_output_aliases=aliases,
        scratch_shapes=[
            pltpu.VMEM((H_A, LANES, LANES), F32),
            pltpu.VMEM((H_B, DK_B, DK_B), F32),
            pltpu.VMEM((HGRN_GROUP * (c // ca), ca, LANES), F32),
        ],
        compiler_params=_cparams(("arbitrary", "arbitrary")),
        name="mixer_state" if has_state else "mixer_prompt",
    )(*args)


def _outproj_kernel(op_ref, os_ref, x_ref, w_ref, gts_ref, gtp_ref, g2_ref, scs_ref, scp_ref,
                    shs_ref, shp_ref, xo_ref, h_ref, *, n_prompt_tiles):
    i = pl.program_id(0)
    half = TM // 2

    def run(o_ref, is_sample):
        for hf in range(2):
            o = jnp.concatenate([o_ref[cb, pl.ds(hf * half, half), :] for cb in range(N_OBLK)],
                                axis=-1)
            y = jnp.dot(o, w_ref[...], preferred_element_type=F32)
            for gl in range(half // GROUP):
                gi = hf * (half // GROUP) + gl
                rows = pl.ds(gi * GROUP, GROUP)
                xn = (x_ref[rows, :] + _mod_row(is_sample, gts_ref, gtp_ref, gi)
                      * y[gl * GROUP:(gl + 1) * GROUP, :])
                xo_ref[rows, :] = xn
                gain = g2_ref[0] * (1.0 + _mod_row(is_sample, scs_ref, scp_ref, gi))
                h_ref[rows, :] = (_rms(xn) * gain
                                  + _mod_row(is_sample, shs_ref, shp_ref, gi)).astype(BF16)

    @pl.when(i < n_prompt_tiles)
    def _():
        run(op_ref, False)

    @pl.when(i >= n_prompt_tiles)
    def _():
        run(os_ref, True)


def _outproj_call(o_p, o_s, x, mod, norm2_g, w_out_l, l, n_prompt):
    t = x.shape[0]
    npt = n_prompt // TM
    kern = functools.partial(_outproj_kernel, n_prompt_tiles=npt)
    last_p = npt - 1
    return pl.pallas_call(
        kern,
        grid=(t // TM,),
        in_specs=[
            pl.BlockSpec((N_OBLK, TM, LANES), lambda i: (0, jnp.minimum(i, last_p), 0)),
            pl.BlockSpec((N_OBLK, TM, LANES), lambda i: (0, jnp.maximum(i - npt, 0), 0)),
            pl.BlockSpec((TM, D_MODEL), lambda i: (i, 0)),
            pl.BlockSpec((D_MODEL, D_MODEL), lambda i: (0, 0)),
            *_mod_specs(GATE1, 1),
            pl.BlockSpec((1, 1, D_MODEL), lambda i: (l, 0, 0)),
            *_mod_specs(SCALE2, 1),
            *_mod_specs(SHIFT2, 1),
        ],
        out_specs=[
            pl.BlockSpec((TM, D_MODEL), lambda i: (i, 0)),
            pl.BlockSpec((TM, D_MODEL), lambda i: (i, 0)),
        ],
        out_shape=[
            jax.ShapeDtypeStruct((t, D_MODEL), F32),
            jax.ShapeDtypeStruct((t, D_MODEL), BF16),
        ],
        compiler_params=_cparams(("arbitrary",)),
        name="out_proj",
    )(o_p, o_s, x, w_out_l, mod, mod, norm2_g, mod, mod, mod, mod)


def _mlp_kernel(h_ref, x_ref, wu_ref, wd_ref, gts_ref, gtp_ref, fg_ref, *rest, nf, final,
                n_prompt_tiles, n_cast):
    cast_in, rest = rest[:n_cast], rest[n_cast:]
    if final:
        yp_ref, ys_ref = rest[:2]
    else:
        xo_ref, r_ref = rest[:2]
    cast_out, acc_ref = rest[2:2 + n_cast], rest[2 + n_cast]
    i = pl.program_id(0)
    f = pl.program_id(1)
    for src, dst in zip(cast_in, cast_out):
        dst[...] = src[0].astype(BF16)

    @pl.when(f == 0)
    def _():
        acc_ref[...] = jnp.zeros_like(acc_ref)

    u = jnp.dot(h_ref[...], wu_ref[...], preferred_element_type=F32)
    u = jnp.square(jnp.maximum(u, 0.0)).astype(BF16)
    acc_ref[...] += jnp.dot(u, wd_ref[...], preferred_element_type=F32)

    def finish(dst_ref, is_sample):
        for gi in range(TM // GROUP):
            rows = pl.ds(gi * GROUP, GROUP)
            xn = x_ref[rows, :] + _mod_row(is_sample, gts_ref, gtp_ref, gi) * acc_ref[rows, :]
            r = lax.rsqrt(jnp.mean(xn * xn, axis=-1, keepdims=True) + EPS)
            if final:
                xn = xn * r * fg_ref[...]
            else:
                r_ref[rows, :] = r
            dst_ref[rows, :] = xn

    @pl.when(f == nf - 1)
    def _():
        if final:
            @pl.when(i < n_prompt_tiles)
            def _():
                finish(yp_ref, False)

            @pl.when(i >= n_prompt_tiles)
            def _():
                finish(ys_ref, True)
        else:
            finish(xo_ref, i >= n_prompt_tiles)


def _mlp_call(h, x, mod, w_up_l, w_down_l, final_g, casts, final, n_prompt, tf=1024):
    t = x.shape[0]
    nf = D_FF // tf
    npt = n_prompt // TM
    kern = functools.partial(_mlp_kernel, nf=nf, final=final, n_prompt_tiles=npt,
                             n_cast=len(casts))
    slab = lambda i, f: jnp.minimum(i * nf + f, npt * nf - 1)
    cast_in_specs, cast_out_specs, cast_shapes = [], [], []
    for w, layer in casts:
        _, rows, cols = w.shape
        rb = rows // (npt * nf)
        cast_in_specs.append(pl.BlockSpec((1, rb, cols),
                                          lambda i, f, layer=layer: (layer, slab(i, f), 0)))
        cast_out_specs.append(pl.BlockSpec((rb, cols), lambda i, f: (slab(i, f), 0)))
        cast_shapes.append(jax.ShapeDtypeStruct((rows, cols), BF16))
    if final:
        last_p = npt - 1
        out_specs = [pl.BlockSpec((TM, D_MODEL), lambda i, f: (jnp.minimum(i, last_p), 0)),
                     pl.BlockSpec((TM, D_MODEL), lambda i, f: (jnp.maximum(i - npt, 0), 0))]
        out_shape = [jax.ShapeDtypeStruct((n_prompt, D_MODEL), F32),
                     jax.ShapeDtypeStruct((t - n_prompt, D_MODEL), F32)]
    else:
        out_specs = [pl.BlockSpec((TM, D_MODEL), lambda i, f: (i, 0)),
                     pl.BlockSpec((TM, 1), lambda i, f: (i, 0))]
        out_shape = [jax.ShapeDtypeStruct((t, D_MODEL), F32), jax.ShapeDtypeStruct((t, 1), F32)]
    return pl.pallas_call(
        kern,
        grid=(t // TM, nf),
        in_specs=[
            pl.BlockSpec((TM, D_MODEL), lambda i, f: (i, 0)),
            pl.BlockSpec((TM, D_MODEL), lambda i, f: (i, 0)),
            pl.BlockSpec((D_MODEL, tf), lambda i, f: (0, f)),
            pl.BlockSpec((tf, D_MODEL), lambda i, f: (f, 0)),
            *_mod_specs(GATE2, 2),
            pl.BlockSpec((1, D_MODEL), lambda i, f: (0, 0)),
            *cast_in_specs,
        ],
        out_specs=out_specs + cast_out_specs,
        out_shape=out_shape + cast_shapes,
        scratch_shapes=[pltpu.VMEM((TM, D_MODEL), F32)],
        compiler_params=_cparams(("arbitrary", "arbitrary")),
        name="mlp_final" if final else "mlp",
    )(h, x, w_up_l, w_down_l, mod, mod, final_g, *[w for w, _ in casts])


def _rope_tables(pos):
    half = DK_B // 2
    inv_freq = 1.0 / (ROPE_BASE ** jnp.linspace(0.0, 1.0, half, dtype=F32))
    ang = pos[:, None] * inv_freq[None, :]
    return jnp.cos(ang), jnp.sin(ang)


def kernel(x_prompt, x_sample, state_hgrn, state_ret, c_prompt, c_sample, lb_logits, w_ada, b_ada,
           norm1_g, norm2_g, w_in, hgrn_norm_g, ret_norm_g, w_out, w_up, w_down, final_g):
    bp, lp, _ = x_prompt.shape
    bs, ls, _ = x_sample.shape
    assert bp == 1 and ls == GROUP and bs == N_SEQ_S and lp % TM == 0
    n_prompt = bp * lp
    n_sample = bs * ls
    c_prompt_chunk = 256
    ca_prompt = 128

    c_all = jnp.concatenate(
        [c_sample, c_prompt, jnp.zeros((MOD_ROWS - bs - bp, D_MODEL), F32)], 0)
    ada_args = _ada_args(c_all, w_ada, b_ada)
    mod = _ada_call(ada_args, 0)

    p = jax.nn.softmax(lb_logits.astype(F32), axis=0)
    cs = jnp.cumsum(p, axis=0)
    lb_all = (cs - cs[0:1]).reshape(DEPTH, H_A, 1, LANES)
    log_gamma = jnp.log1p(-jnp.exp2(-5.0 - jnp.arange(H_B, dtype=F32)))
    ghn = hgrn_norm_g.reshape(DEPTH, H_A, 1, LANES)
    grn = ret_norm_g.reshape(DEPTH, H_B, 1, DK_B)
    n1 = norm1_g.reshape(DEPTH, 1, D_MODEL)
    n2 = norm2_g.reshape(DEPTH, 1, D_MODEL)
    fg = final_g.reshape(1, D_MODEL)

    cos_p, sin_p = _rope_tables(jnp.arange(lp, dtype=F32))
    cos_s, sin_s = _rope_tables(PAST_LEN + jnp.arange(ls, dtype=F32))
    tab_p = _mixer_tables(c_prompt_chunk, ca_prompt, log_gamma)
    tab_s = _mixer_tables(ls, ls, log_gamma)

    w_in_l = w_in[0].astype(BF16)
    new_p = (jnp.zeros((DEPTH, bp, H_A, LANES, LANES), F32), jnp.zeros((DEPTH, bp, H_B, DK_B, DK_B), F32))
    new_s = (jnp.zeros((DEPTH, bs, H_A, LANES, LANES), F32), jnp.zeros((DEPTH, bs, H_B, DK_B, DK_B), F32))
    for l in range(DEPTH):
        if l == 0:
            proj, x = _inproj_first_call(x_prompt.reshape(n_prompt, D_MODEL),
                                         x_sample.reshape(n_sample, D_MODEL), mod, n1, w_in_l, l)
        else:
            proj, w_up_l, w_down_l = _inproj_call(x, r, mod, n1, w_in_l,
                                                  [(w_up, l), (w_down, l)], l, n_prompt)
        more = l + 1 < DEPTH
        casts = [(w_out, l)] + ([(w_up, l), (w_down, l)] if l == 0 else [])
        res = _mixer_call(proj, cos_p, sin_p, lb_all[l], ghn[l], grn[l], tab_p, None, new_p, casts,
                          (ada_args, l + 1) if more else None,
                          c=c_prompt_chunk, ca=ca_prompt, batch=bp,
                          n_chunks=lp // c_prompt_chunk, row0=0, l=l)
        o_p, w_out_l = res[0], res[3]
        new_p = (res[1], res[2])
        if l == 0:
            w_up_l, w_down_l = res[4], res[5]
        if more:
            mod_next = res[3 + len(casts)]
        res = _mixer_call(proj, cos_s, sin_s, lb_all[l], ghn[l], grn[l], tab_s,
                          (state_hgrn, state_ret), new_s, [], None,
                          c=ls, ca=ls, batch=bs, n_chunks=1, row0=n_prompt, l=l)
        o_s = res[0]
        new_s = (res[1], res[2])
        x, h2 = _outproj_call(o_p, o_s, x, mod, n2, w_out_l, l, n_prompt)
        res = _mlp_call(h2, x, mod, w_up_l, w_down_l, fg, [(w_in, l + 1)] if more else [],
                        not more, n_prompt)
        if more:
            x, r, w_in_l = res
            mod = mod_next

    y_p, y_s = res
    return (y_p.reshape(bp, lp, D_MODEL), y_s.reshape(bs, ls, D_MODEL),
            new_p[0], new_p[1], new_s[0], new_s[1])
```

```python
import functools

import numpy as np
import jax
import jax.numpy as jnp
from jax import lax
from jax.experimental import pallas as pl
from jax.experimental.pallas import tpu as pltpu

F32 = jnp.float32
BF16 = jnp.bfloat16

D_MODEL = 2048
DEPTH = 4
H_A = 8
H_B = 4
DK_B = 256
N_PROJ = 8192
N_MOD = 6
D_FF = 4 * D_MODEL
EPS = 1e-6
ROPE_BASE = 10000.0
LOG2_E = 1.4426950408889634
PAST_LEN = 2048

LANES = 128
N_PBLK = N_PROJ // LANES
N_OBLK = D_MODEL // LANES
GROUP = 64
TM = 512
N_SEQ_S = TM // GROUP
MOD_ROWS = 16
HGRN_GROUP = 8

VMEM_LIMIT = 56 * 1024 * 1024

_NT = (((1,), (1,)), ((), ()))
_TN = (((0,), (0,)), ((), ()))

SHIFT1, SCALE1, GATE1, SHIFT2, SCALE2, GATE2 = range(N_MOD)


def _cparams(sem):
    return pltpu.CompilerParams(dimension_semantics=sem, vmem_limit_bytes=VMEM_LIMIT)


def _silu(x):
    hx = 0.5 * x
    return hx * jnp.tanh(hx) + hx


def _rms(x):
    return x * lax.rsqrt(jnp.mean(x * x, axis=-1, keepdims=True) + EPS)


def _mod_specs(which, ngrid):
    if ngrid == 3:
        return [pl.BlockSpec((1, N_SEQ_S, 1, D_MODEL), lambda i, j, k: (0, 0, 0, which)),
                pl.BlockSpec((1, 1, 1, D_MODEL), lambda i, j, k: (0, N_SEQ_S, 0, which))]
    if ngrid == 1:
        return [pl.BlockSpec((1, N_SEQ_S, 1, D_MODEL), lambda i: (0, 0, 0, which)),
                pl.BlockSpec((1, 1, 1, D_MODEL), lambda i: (0, N_SEQ_S, 0, which))]
    return [pl.BlockSpec((1, N_SEQ_S, 1, D_MODEL), lambda i, j: (0, 0, 0, which)),
            pl.BlockSpec((1, 1, 1, D_MODEL), lambda i, j: (0, N_SEQ_S, 0, which))]


def _mod_row(is_sample, s_ref, p_ref, gi):
    if isinstance(is_sample, bool):
        return s_ref[0, gi] if is_sample else p_ref[0, 0]
    return jnp.where(is_sample, s_ref[0, gi], p_ref[0, 0])


def _ada_kernel(c_ref, w_ref, b_ref, o_ref):
    c = _silu(c_ref[...]).astype(BF16)
    w = w_ref[0].astype(BF16)
    res = jnp.dot(c, w, preferred_element_type=F32) + b_ref[0]
    for r in range(MOD_ROWS):
        o_ref[0, r] = res[r:r + 1, :]


def _ada_args(c_all, w_ada, b_ada):
    return c_all, w_ada, b_ada.reshape(DEPTH, 1, N_MOD * D_MODEL)


def _ada_call(ada_args, l, tn=1024):
    n = N_MOD * D_MODEL
    return pl.pallas_call(
        _ada_kernel,
        grid=(n // tn,),
        in_specs=[
            pl.BlockSpec((MOD_ROWS, D_MODEL), lambda j: (0, 0)),
            pl.BlockSpec((1, D_MODEL, tn), lambda j: (l, 0, j)),
            pl.BlockSpec((1, 1, tn), lambda j: (l, 0, j)),
        ],
        out_specs=pl.BlockSpec((1, MOD_ROWS, 1, tn), lambda j: (0, 0, 0, j)),
        out_shape=jax.ShapeDtypeStruct((1, MOD_ROWS, 1, n), F32),
        compiler_params=_cparams(("arbitrary",)),
        name="ada_mod",
    )(*ada_args)


def _inproj_tile(x_ref, r_ref, g_ref, scs_ref, scp_ref, shs_ref, shp_ref, w_ref, o_ref, is_sample,
                 tn):
    g = g_ref[0]
    hs = []
    for gi in range(TM // GROUP):
        rows = pl.ds(gi * GROUP, GROUP)
        gain = g * (1.0 + _mod_row(is_sample, scs_ref, scp_ref, gi))
        shift = _mod_row(is_sample, shs_ref, shp_ref, gi)
        hs.append((x_ref[rows, :] * r_ref[rows, :] * gain + shift).astype(BF16))
    h = jnp.concatenate(hs, axis=0)
    acc = jnp.dot(h, w_ref[...], preferred_element_type=F32)
    for c in range(tn // LANES):
        o_ref[c] = acc[:, c * LANES:(c + 1) * LANES].astype(BF16)


def _inproj_first_kernel(xp_ref, xs_ref, g_ref, scs_ref, scp_ref, shs_ref, shp_ref, w_ref, o_ref,
                         x_ref, r_scr, *, tn, n_prompt_tiles):
    is_sample = pl.program_id(0) >= n_prompt_tiles

    @pl.when(pl.program_id(1) == 0)
    def _():
        @pl.when(is_sample)
        def _():
            x_ref[...] = xs_ref[...]

        @pl.when(jnp.logical_not(is_sample))
        def _():
            x_ref[...] = xp_ref[...]
        x = x_ref[...]
        r_scr[...] = lax.rsqrt(jnp.mean(x * x, axis=-1, keepdims=True) + EPS)

    _inproj_tile(x_ref, r_scr, g_ref, scs_ref, scp_ref, shs_ref, shp_ref, w_ref, o_ref, is_sample,
                 tn)


def _inproj_kernel(x_ref, r_ref, g_ref, scs_ref, scp_ref, shs_ref, shp_ref, w_ref, *rest,
                   tn, n_prompt_tiles, n_cast):
    cast_in, o_ref, cast_out = rest[:n_cast], rest[n_cast], rest[n_cast + 1:]
    for src, dst in zip(cast_in, cast_out):
        dst[...] = src[0].astype(BF16)
    is_sample = pl.program_id(1) >= n_prompt_tiles
    _inproj_tile(x_ref, r_ref, g_ref, scs_ref, scp_ref, shs_ref, shp_ref, w_ref, o_ref, is_sample,
                 tn)


def _inproj_first_call(x_prompt, x_sample, mod, norm1_g, w_in_l, l, tn=2048):
    n_prompt = x_prompt.shape[0]
    t = n_prompt + x_sample.shape[0]
    npt = n_prompt // TM
    kern = functools.partial(_inproj_first_kernel, tn=tn, n_prompt_tiles=npt)
    tile = pl.BlockSpec((TM, D_MODEL), lambda i, j: (i, 0))
    return pl.pallas_call(
        kern,
        grid=(t // TM, N_PROJ // tn),
        in_specs=[
            pl.BlockSpec((TM, D_MODEL), lambda i, j: (jnp.minimum(i, npt - 1), 0)),
            pl.BlockSpec((TM, D_MODEL), lambda i, j: (jnp.maximum(i - npt, 0), 0)),
            pl.BlockSpec((1, 1, D_MODEL), lambda i, j: (l, 0, 0)),
            *_mod_specs(SCALE1, 2),
            *_mod_specs(SHIFT1, 2),
            pl.BlockSpec((D_MODEL, tn), lambda i, j: (0, j)),
        ],
        out_specs=[pl.BlockSpec((tn // LANES, TM, LANES), lambda i, j: (j, i, 0)), tile],
        out_shape=[jax.ShapeDtypeStruct((N_PBLK, t, LANES), BF16),
                   jax.ShapeDtypeStruct((t, D_MODEL), F32)],
        scratch_shapes=[pltpu.VMEM((TM, 1), F32)],
        compiler_params=_cparams(("arbitrary", "arbitrary")),
        name="in_proj_first",
    )(x_prompt, x_sample, norm1_g, mod, mod, mod, mod, w_in_l)


def _inproj_call(x, r, mod, norm1_g, w_in_l, casts, l, n_prompt, tn=2048):
    t = x.shape[0]
    npt = n_prompt // TM
    n_slab = (N_PROJ // tn) * npt
    kern = functools.partial(_inproj_kernel, tn=tn, n_prompt_tiles=npt, n_cast=len(casts))
    slab = lambda j, i: j * npt + jnp.minimum(i, npt - 1)
    in_specs = [
        pl.BlockSpec((TM, D_MODEL), lambda j, i: (i, 0)),
        pl.BlockSpec((TM, 1), lambda j, i: (i, 0)),
        pl.BlockSpec((1, 1, D_MODEL), lambda j, i: (l, 0, 0)),
        *_mod_specs(SCALE1, 2),
        *_mod_specs(SHIFT1, 2),
        pl.BlockSpec((D_MODEL, tn), lambda j, i: (0, j)),
    ]
    args = [x, r, norm1_g, mod, mod, mod, mod, w_in_l]
    out_specs = [pl.BlockSpec((tn // LANES, TM, LANES), lambda j, i: (j, i, 0))]
    out_shape = [jax.ShapeDtypeStruct((N_PBLK, t, LANES), BF16)]
    for w, layer in casts:
        _, rows, cols = w.shape
        rb = rows // n_slab
        in_specs.append(pl.BlockSpec((1, rb, cols),
                                     lambda j, i, layer=layer: (layer, slab(j, i), 0)))
        args.append(w)
        out_specs.append(pl.BlockSpec((rb, cols), lambda j, i: (slab(j, i), 0)))
        out_shape.append(jax.ShapeDtypeStruct((rows, cols), BF16))
    return pl.pallas_call(
        kern,
        grid=(N_PROJ // tn, t // TM),
        in_specs=in_specs,
        out_specs=out_specs,
        out_shape=out_shape,
        compiler_params=_cparams(("arbitrary", "arbitrary")),
        name="in_proj",
    )(*args)


def _hgrn_ref_rows(b_scr, lvl, c):
    half = 1 << lvl
    m = 2 * half
    if half >= 8:
        blocks = [jnp.broadcast_to(b_scr[pl.ds(j * m + half - 1, 1), :], (m, LANES))
                  for j in range(c // m)]
    elif half == 4:
        blocks = [jnp.broadcast_to(b_scr[pl.ds(8 * j + 3, 1), :], (8, LANES))
                  for j in range(c // 8)]
    else:
        sub = lax.broadcasted_iota(jnp.int32, (8, LANES), 0)
        blocks = [jnp.where(sub < 4,
                            jnp.broadcast_to(b_scr[pl.ds(8 * j + 1, 1), :], (8, LANES)),
                            jnp.broadcast_to(b_scr[pl.ds(8 * j + 5, 1), :], (8, LANES)))
                  for j in range(c // 8)]
    return blocks[0] if len(blocks) == 1 else jnp.concatenate(blocks, axis=0)


def _pick_rows(row, q, k, lvl, c):
    half = 1 << lvl
    if half < 8:
        return jnp.where((row & half) != 0, q, k)
    return jnp.concatenate([(q if j & 1 else k)[j * half:(j + 1) * half]
                            for j in range(c // half)], axis=0)


def _mixer_kernel(*refs, c, ca, n_chunks, has_state, n_alias, n_cast, with_ada):
    refs = list(refs)
    (proj_ref, cos_ref, sin_ref, lb_ref, ghn_ref, grn_ref, lv_ref, tri_ref, dmat_ref, xi_ref,
     zeta_ref, gc_ref) = refs[:12]
    pos = 12
    if has_state:
        sa_in_ref, sb_in_ref = refs[pos:pos + 2]
        pos += 2
    pos += n_alias
    cast_in = refs[pos:pos + n_cast]
    pos += n_cast
    ada_in = refs[pos:pos + 3 * with_ada]
    pos += 3 * with_ada
    o_ref, sa_out_ref, sb_out_ref = refs[pos:pos + 3]
    pos += 3
    cast_out = refs[pos:pos + n_cast]
    pos += n_cast
    ada_out = refs[pos:pos + with_ada]
    pos += with_ada
    st_scr, sb_scr, b_scr = refs[pos:]

    ci = pl.program_id(1)
    n_lvl = ca.bit_length() - 1

    @pl.when(ci == 0)
    def _():
        if has_state:
            for h in range(H_A):
                st_scr[h] = sa_in_ref[0, 0, h].T
            sb_scr[...] = sb_in_ref[0, 0]
        else:
            st_scr[...] = jnp.zeros_like(st_scr)
            sb_scr[...] = jnp.zeros_like(sb_scr)

    if with_ada:
        _ada_kernel(*ada_in, *ada_out)
    for src, dst in zip(cast_in, cast_out):
        dst[...] = src[0].astype(BF16)

    row = lax.broadcasted_iota(jnp.int32, (ca, LANES), 0)

    n_sub = c // ca

    def hgrn_group(heads):
        streams = [(h, sub) for h in heads for sub in range(n_sub)]
        vals = []
        for h, sub in streams:
            rows = pl.ds(sub * ca, ca)
            lb = lb_ref[h]
            f = lb + (1.0 - lb) * jax.nn.sigmoid(proj_ref[H_A + h, rows, :].astype(F32))
            logf = jnp.log(f) * LOG2_E
            hi = logf.astype(BF16)
            lo = (logf - hi.astype(F32)).astype(BF16)
            b2 = jnp.dot(tri_ref[...], jnp.concatenate([hi, lo], axis=-1),
                         preferred_element_type=F32)
            vals.append(dict(f=f, b2=b2))

        for si, (h, sub) in enumerate(streams):
            d = vals[si]
            f, b2 = d["f"], d["b2"]
            q = _silu(proj_ref[h, pl.ds(sub * ca, ca), :].astype(F32))
            k = 1.0 - f
            b = b2[:, :LANES] + b2[:, LANES:]
            b_sub = b_scr.at[si]
            b_sub[...] = b
            z = jnp.where((row & 1) != 0, q * f, k).astype(BF16)
            p = lax.dot_general(z, z, _NT, preferred_element_type=F32)
            a = jnp.where(lv_ref[...] == 0, p.astype(BF16), jnp.zeros((), BF16))
            for lvl in range(1, n_lvl):
                e = jnp.exp2(-jnp.abs(b - _hgrn_ref_rows(b_sub, lvl, ca)))
                z = (_pick_rows(row, q, k, lvl, ca) * e).astype(BF16)
                p = lax.dot_general(z, z, _NT, preferred_element_type=F32)
                a = jnp.where(lv_ref[...] == lvl, p.astype(BF16), a)
            vb = proj_ref[2 * H_A + h, pl.ds(sub * ca, ca), :]
            b_last = b[ca - 1:ca, :]
            kd = (k * jnp.exp2(b_last - b)).astype(BF16)
            d.update(a=a, upd=lax.dot_general(vb, kd, _TN, preferred_element_type=F32),
                     decay=jnp.exp2(b_last), qi=(q * jnp.exp2(b)).astype(BF16),
                     diag=jnp.sum(q * k, axis=-1, keepdims=True))

        for gi, h in enumerate(heads):
            s_cur = st_scr[h]
            for sub in range(n_sub):
                d = vals[gi * n_sub + sub]
                vb = proj_ref[2 * H_A + h, pl.ds(sub * ca, ca), :]
                o = jnp.dot(d["a"], vb, preferred_element_type=F32)
                d["o"] = o + lax.dot_general(d["qi"], s_cur.astype(BF16), _NT,
                                             preferred_element_type=F32)
                s_cur = s_cur * d["decay"] + d["upd"]
            st_scr[h] = s_cur

        for si, (h, sub) in enumerate(streams):
            rows = pl.ds(sub * ca, ca)
            d = vals[si]
            o = d["o"] + d["diag"] * proj_ref[2 * H_A + h, rows, :].astype(F32)
            ga = proj_ref[3 * H_A + h, rows, :].astype(F32)
            o_ref[h, rows, :] = (_rms(o) * ghn_ref[h] * _silu(ga)).astype(BF16)

    for g0 in range(0, H_A, HGRN_GROUP):
        hgrn_group(list(range(g0, g0 + HGRN_GROUP)))

    cos = cos_ref[...]
    sin = sin_ref[...]

    rvals = []
    for hb in range(H_B):
        def pair(base):
            return (proj_ref[base + 2 * hb].astype(F32), proj_ref[base + 2 * hb + 1].astype(F32))

        q1, q2 = pair(4 * H_A)
        k1, k2 = pair(4 * H_A + 2 * H_B)
        vbase = 4 * H_A + 4 * H_B
        vb = jnp.concatenate([proj_ref[vbase + 2 * hb], proj_ref[vbase + 2 * hb + 1]], axis=-1)
        q = jnp.concatenate([q1 * cos - q2 * sin, q2 * cos + q1 * sin], axis=-1)
        k = jnp.concatenate([k1 * cos - k2 * sin, k2 * cos + k1 * sin], axis=-1) * (DK_B ** -0.5)
        qb = q.astype(BF16)
        s = sb_scr[hb]
        qk = lax.dot_general(qb, k.astype(BF16), _NT, preferred_element_type=F32)
        qs = jnp.dot(qb, s.astype(BF16), preferred_element_type=F32)
        kz = (k * zeta_ref[hb]).astype(BF16)
        sb_scr[hb] = gc_ref[hb] * s + lax.dot_general(kz, vb, _TN, preferred_element_type=F32)
        rvals.append((qk, qs, vb))
    for hb in range(H_B):
        qk, qs, vb = rvals[hb]
        o = jnp.dot((qk * dmat_ref[hb]).astype(BF16), vb, preferred_element_type=F32)
        o = o + qs * xi_ref[hb]
        gbase = 4 * H_A + 6 * H_B
        gate = jnp.concatenate([proj_ref[gbase + 2 * hb].astype(F32),
                                proj_ref[gbase + 2 * hb + 1].astype(F32)], axis=-1)
        out = _rms(o) * grn_ref[hb] * _silu(gate)
        o_ref[H_A + 2 * hb] = out[:, :LANES].astype(BF16)
        o_ref[H_A + 2 * hb + 1] = out[:, LANES:].astype(BF16)

    @pl.when(ci == n_chunks - 1)
    def _():
        for h in range(H_A):
            sa_out_ref[0, 0, h] = st_scr[h].T
        sb_out_ref[0, 0] = sb_scr[...]


def _mixer_tables(c, ca, log_gamma):
    t = np.arange(ca)
    x = t[:, None] ^ t[None, :]
    lv = np.where(t[:, None] > t[None, :], np.floor(np.log2(np.maximum(x, 1))), -1).astype(np.int32)
    tri = (t[:, None] >= t[None, :]).astype(np.float32)
    idx = jnp.arange(c, dtype=F32)
    lg = log_gamma[:, None, None]
    causal = jnp.asarray(np.tril(np.ones((c, c), dtype=bool)))
    dmat = jnp.where(causal[None], jnp.exp((idx[:, None] - idx[None, :])[None] * lg), 0.0)
    xi = jnp.exp((idx + 1.0)[None, :] * log_gamma[:, None])[..., None]
    zeta = jnp.exp((c - 1.0 - idx)[None, :] * log_gamma[:, None])[..., None]
    g_c = jnp.exp(c * log_gamma)[:, None, None]
    return dict(
        lv=jnp.asarray(lv, dtype=BF16), tri=jnp.asarray(tri, dtype=BF16), dmat=dmat,
        xi=jnp.broadcast_to(xi, (H_B, c, DK_B)), zeta=jnp.broadcast_to(zeta, (H_B, c, DK_B)),
        gc=jnp.broadcast_to(g_c, (H_B, 1, DK_B)))


def _mixer_call(proj, cos, sin, lb, ghn, grn, tab, states, prev_states, casts, ada, *, c, ca, batch,
                n_chunks, row0, l):
    has_state = states is not None
    blk0 = row0 // c
    const3 = lambda b, i: (0, 0, 0)
    sa_spec = pl.BlockSpec((1, 1, H_A, LANES, LANES), lambda b, i: (l, b, 0, 0, 0))
    sb_spec = pl.BlockSpec((1, 1, H_B, DK_B, DK_B), lambda b, i: (l, b, 0, 0, 0))
    in_specs = [
        pl.BlockSpec((N_PBLK, c, LANES), lambda b, i: (0, blk0 + b * n_chunks + i, 0)),
        pl.BlockSpec((c, LANES), lambda b, i: (i, 0)),
        pl.BlockSpec((c, LANES), lambda b, i: (i, 0)),
        pl.BlockSpec((H_A, 1, LANES), const3),
        pl.BlockSpec((H_A, 1, LANES), const3),
        pl.BlockSpec((H_B, 1, DK_B), const3),
        pl.BlockSpec((ca, ca), lambda b, i: (0, 0)),
        pl.BlockSpec((ca, ca), lambda b, i: (0, 0)),
        pl.BlockSpec((H_B, c, c), const3),
        pl.BlockSpec((H_B, c, DK_B), const3),
        pl.BlockSpec((H_B, c, DK_B), const3),
        pl.BlockSpec((H_B, 1, DK_B), const3),
    ]
    args = [proj, cos, sin, lb, ghn, grn, tab["lv"], tab["tri"], tab["dmat"], tab["xi"],
            tab["zeta"], tab["gc"]]
    if has_state:
        in_specs += [sa_spec, sb_spec]
        args += list(states)
    aliases = {}
    if prev_states is not None:
        for k, arr in enumerate(prev_states):
            aliases[len(args)] = 1 + k
            in_specs.append(pl.BlockSpec(memory_space=pl.ANY))
            args.append(arr)
    out_specs = [
        pl.BlockSpec((N_OBLK, c, LANES), lambda b, i: (0, b * n_chunks + i, 0)), sa_spec, sb_spec]
    out_shape = [
        jax.ShapeDtypeStruct((N_OBLK, batch * n_chunks * c, LANES), BF16),
        jax.ShapeDtypeStruct((DEPTH, batch, H_A, LANES, LANES), F32),
        jax.ShapeDtypeStruct((DEPTH, batch, H_B, DK_B, DK_B), F32),
    ]
    assert not casts or batch == 1
    for w, layer in casts:
        _, r, cols = w.shape
        rb = r // n_chunks
        in_specs.append(pl.BlockSpec((1, rb, cols), lambda b, i, layer=layer: (layer, i, 0)))
        args.append(w)
        out_specs.append(pl.BlockSpec((rb, cols), lambda b, i: (i, 0)))
        out_shape.append(jax.ShapeDtypeStruct((r, cols), BF16))
    if ada is not None:
        assert batch == 1
        ada_args, layer = ada
        n = N_MOD * D_MODEL
        tn = n // n_chunks
        in_specs += [pl.BlockSpec((MOD_ROWS, D_MODEL), lambda b, i: (0, 0)),
                     pl.BlockSpec((1, D_MODEL, tn), lambda b, i: (layer, 0, i)),
                     pl.BlockSpec((1, 1, tn), lambda b, i: (layer, 0, i))]
        args += list(ada_args)
        out_specs.append(pl.BlockSpec((1, MOD_ROWS, 1, tn), lambda b, i: (0, 0, 0, i)))
        out_shape.append(jax.ShapeDtypeStruct((1, MOD_ROWS, 1, n), F32))
    kern = functools.partial(_mixer_kernel, c=c, ca=ca, n_chunks=n_chunks, has_state=has_state,
                             n_alias=len(aliases), n_cast=len(casts), with_ada=int(ada is not None))
    return pl.pallas_call(
        kern,
        grid=(batch, n_chunks),
        in_specs=in_specs,
        out_specs=out_specs,
        out_shape=out_shape,
        input_output_aliases=aliases,
        scratch_shapes=[
            pltpu.VMEM((H_A, LANES, LANES), F32),
            pltpu.VMEM((H_B, DK_B, DK_B), F32),
            pltpu.VMEM((HGRN_GROUP * (c // ca), ca, LANES), F32),
        ],
        compiler_params=_cparams(("arbitrary", "arbitrary")),
        name="mixer_state" if has_state else "mixer_prompt",
    )(*args)


def _outproj_kernel(op_ref, os_ref, x_ref, w_ref, gts_ref, gtp_ref, g2_ref, scs_ref, scp_ref,
                    shs_ref, shp_ref, xo_ref, h_ref, *, n_prompt_tiles):
    i = pl.program_id(0)
    half = TM // 2

    def run(o_ref, is_sample):
        for hf in range(2):
            o = jnp.concatenate([o_ref[cb, pl.ds(hf * half, half), :] for cb in range(N_OBLK)],
                                axis=-1)
            y = jnp.dot(o, w_ref[...], preferred_element_type=F32)
            for gl in range(half // GROUP):
                gi = hf * (half // GROUP) + gl
                rows = pl.ds(gi * GROUP, GROUP)
                xn = (x_ref[rows, :] + _mod_row(is_sample, gts_ref, gtp_ref, gi)
                      * y[gl * GROUP:(gl + 1) * GROUP, :])
                xo_ref[rows, :] = xn
                gain = g2_ref[0] * (1.0 + _mod_row(is_sample, scs_ref, scp_ref, gi))
                h_ref[rows, :] = (_rms(xn) * gain
                                  + _mod_row(is_sample, shs_ref, shp_ref, gi)).astype(BF16)

    @pl.when(i < n_prompt_tiles)
    def _():
        run(op_ref, False)

    @pl.when(i >= n_prompt_tiles)
    def _():
        run(os_ref, True)


def _outproj_call(o_p, o_s, x, mod, norm2_g, w_out_l, l, n_prompt):
    t = x.shape[0]
    npt = n_prompt // TM
    kern = functools.partial(_outproj_kernel, n_prompt_tiles=npt)
    last_p = npt - 1
    return pl.pallas_call(
        kern,
        grid=(t // TM,),
        in_specs=[
            pl.BlockSpec((N_OBLK, TM, LANES), lambda i: (0, jnp.minimum(i, last_p), 0)),
            pl.BlockSpec((N_OBLK, TM, LANES), lambda i: (0, jnp.maximum(i - npt, 0), 0)),
            pl.BlockSpec((TM, D_MODEL), lambda i: (i, 0)),
            pl.BlockSpec((D_MODEL, D_MODEL), lambda i: (0, 0)),
            *_mod_specs(GATE1, 1),
            pl.BlockSpec((1, 1, D_MODEL), lambda i: (l, 0, 0)),
            *_mod_specs(SCALE2, 1),
            *_mod_specs(SHIFT2, 1),
        ],
        out_specs=[
            pl.BlockSpec((TM, D_MODEL), lambda i: (i, 0)),
            pl.BlockSpec((TM, D_MODEL), lambda i: (i, 0)),
        ],
        out_shape=[
            jax.ShapeDtypeStruct((t, D_MODEL), F32),
            jax.ShapeDtypeStruct((t, D_MODEL), BF16),
        ],
        compiler_params=_cparams(("arbitrary",)),
        name="out_proj",
    )(o_p, o_s, x, w_out_l, mod, mod, norm2_g, mod, mod, mod, mod)


def _mlp_kernel(h_ref, x_ref, wu_ref, wd_ref, gts_ref, gtp_ref, fg_ref, *rest, nf, final,
                n_tiles):
    if final:
        yp_ref, ys_ref, acc_ref = rest
    else:
        xo_ref, r_ref, acc_ref = rest
    f = pl.program_id(1)
    s = pl.program_id(2)
    tile = 2 * pl.program_id(0) + s
    last_tile = n_tiles - 1

    def finish(dst_ref, is_sample):
        for gi in range(TM // GROUP):
            rows = pl.ds(gi * GROUP, GROUP)
            xn = x_ref[rows, :] + _mod_row(is_sample, gts_ref, gtp_ref, gi) * acc_ref[s, rows, :]
            r = lax.rsqrt(jnp.mean(xn * xn, axis=-1, keepdims=True) + EPS)
            if final:
                xn = xn * r * fg_ref[...]
            else:
                r_ref[rows, :] = r
            dst_ref[rows, :] = xn

    @pl.when(tile <= last_tile)
    def _():
        @pl.when(f == 0)
        def _():
            acc_ref[s] = jnp.zeros((TM, D_MODEL), F32)

        u = jnp.dot(h_ref[...], wu_ref[...], preferred_element_type=F32)
        u = jnp.square(jnp.maximum(u, 0.0)).astype(BF16)
        acc_ref[s] += jnp.dot(u, wd_ref[...], preferred_element_type=F32)

        @pl.when(f == nf - 1)
        def _():
            if final:
                @pl.when(tile < last_tile)
                def _():
                    finish(yp_ref, False)

                @pl.when(tile == last_tile)
                def _():
                    finish(ys_ref, True)
            else:
                finish(xo_ref, tile == last_tile)


def _mlp_call(h, x, mod, w_up_l, w_down_l, final_g, l, final, n_prompt, tf=1024):
    t = x.shape[0]
    nf = D_FF // tf
    n_tiles = t // TM
    last = n_tiles - 1
    kern = functools.partial(_mlp_kernel, nf=nf, final=final, n_tiles=n_tiles)
    tile = lambda p, s: jnp.minimum(2 * p + s, last)
    late = lambda p, f, s: jnp.where(f == nf - 1, tile(p, s), tile(p, 0))
    if final:
        out_specs = [pl.BlockSpec((TM, D_MODEL), lambda p, f, s: (jnp.minimum(late(p, f, s), last - 1), 0)),
                     pl.BlockSpec((TM, D_MODEL), lambda p, f, s: (0, 0))]
        out_shape = [jax.ShapeDtypeStruct((n_prompt, D_MODEL), F32),
                     jax.ShapeDtypeStruct((t - n_prompt, D_MODEL), F32)]
    else:
        out_specs = [pl.BlockSpec((TM, D_MODEL), lambda p, f, s: (late(p, f, s), 0)),
                     pl.BlockSpec((TM, 1), lambda p, f, s: (late(p, f, s), 0))]
        out_shape = [jax.ShapeDtypeStruct((t, D_MODEL), F32), jax.ShapeDtypeStruct((t, 1), F32)]
    return pl.pallas_call(
        kern,
        grid=((n_tiles + 1) // 2, nf, 2),
        in_specs=[
            pl.BlockSpec((TM, D_MODEL), lambda p, f, s: (tile(p, s), 0)),
            pl.BlockSpec((TM, D_MODEL), lambda p, f, s: (late(p, f, s), 0)),
            pl.BlockSpec((D_MODEL, tf), lambda p, f, s: (0, f)),
            pl.BlockSpec((tf, D_MODEL), lambda p, f, s: (f, 0)),
            *_mod_specs(GATE2, 3),
            pl.BlockSpec((1, D_MODEL), lambda p, f, s: (0, 0)),
        ],
        out_specs=out_specs,
        out_shape=out_shape,
        scratch_shapes=[pltpu.VMEM((2, TM, D_MODEL), F32)],
        compiler_params=_cparams(("arbitrary", "arbitrary", "arbitrary")),
        name="mlp_final" if final else "mlp",
    )(h, x, w_up_l, w_down_l, mod, mod, final_g)


def _rope_tables(pos):
    half = DK_B // 2
    inv_freq = 1.0 / (ROPE_BASE ** jnp.linspace(0.0, 1.0, half, dtype=F32))
    ang = pos[:, None] * inv_freq[None, :]
    return jnp.cos(ang), jnp.sin(ang)


def kernel(x_prompt, x_sample, state_hgrn, state_ret, c_prompt, c_sample, lb_logits, w_ada, b_ada,
           norm1_g, norm2_g, w_in, hgrn_norm_g, ret_norm_g, w_out, w_up, w_down, final_g):
    bp, lp, _ = x_prompt.shape
    bs, ls, _ = x_sample.shape
    assert bp == 1 and ls == GROUP and bs == N_SEQ_S and lp % TM == 0
    n_prompt = bp * lp
    n_sample = bs * ls
    c_prompt_chunk = 256
    ca_prompt = 128

    c_all = jnp.concatenate(
        [c_sample, c_prompt, jnp.zeros((MOD_ROWS - bs - bp, D_MODEL), F32)], 0)
    ada_args = _ada_args(c_all, w_ada, b_ada)
    mod = _ada_call(ada_args, 0)

    p = jax.nn.softmax(lb_logits.astype(F32), axis=0)
    cs = jnp.cumsum(p, axis=0)
    lb_all = (cs - cs[0:1]).reshape(DEPTH, H_A, 1, LANES)
    log_gamma = jnp.log1p(-jnp.exp2(-5.0 - jnp.arange(H_B, dtype=F32)))
    ghn = hgrn_norm_g.reshape(DEPTH, H_A, 1, LANES)
    grn = ret_norm_g.reshape(DEPTH, H_B, 1, DK_B)
    n1 = norm1_g.reshape(DEPTH, 1, D_MODEL)
    n2 = norm2_g.reshape(DEPTH, 1, D_MODEL)
    fg = final_g.reshape(1, D_MODEL)

    cos_p, sin_p = _rope_tables(jnp.arange(lp, dtype=F32))
    cos_s, sin_s = _rope_tables(PAST_LEN + jnp.arange(ls, dtype=F32))
    tab_p = _mixer_tables(c_prompt_chunk, ca_prompt, log_gamma)
    tab_s = _mixer_tables(ls, ls, log_gamma)

    w_in_l = w_in[0].astype(BF16)
    new_p = (jnp.zeros((DEPTH, bp, H_A, LANES, LANES), F32), jnp.zeros((DEPTH, bp, H_B, DK_B, DK_B), F32))
    new_s = (jnp.zeros((DEPTH, bs, H_A, LANES, LANES), F32), jnp.zeros((DEPTH, bs, H_B, DK_B, DK_B), F32))
    for l in range(DEPTH):
        if l == 0:
            proj, x = _inproj_first_call(x_prompt.reshape(n_prompt, D_MODEL),
                                         x_sample.reshape(n_sample, D_MODEL), mod, n1, w_in_l, l)
        else:
            proj, w_up_l, w_down_l = _inproj_call(x, r, mod, n1, w_in_l,
                                                  [(w_up, l), (w_down, l)], l, n_prompt)
        more = l + 1 < DEPTH
        casts = [(w_out, l)] + ([(w_up, l), (w_down, l)] if l == 0 else [])
        casts += [(w_in, l + 1)] if more else []
        res = _mixer_call(proj, cos_p, sin_p, lb_all[l], ghn[l], grn[l], tab_p, None, new_p, casts,
                          (ada_args, l + 1) if more else None,
                          c=c_prompt_chunk, ca=ca_prompt, batch=bp,
                          n_chunks=lp // c_prompt_chunk, row0=0, l=l)
        o_p, w_out_l = res[0], res[3]
        new_p = (res[1], res[2])
        if l == 0:
            w_up_l, w_down_l = res[4], res[5]
        if more:
            w_in_l, mod_next = res[3 + len(casts) - 1], res[3 + len(casts)]
        res = _mixer_call(proj, cos_s, sin_s, lb_all[l], ghn[l], grn[l], tab_s,
                          (state_hgrn, state_ret), new_s, [], None,
                          c=ls, ca=ls, batch=bs, n_chunks=1, row0=n_prompt, l=l)
        o_s = res[0]
        new_s = (res[1], res[2])
        x, h2 = _outproj_call(o_p, o_s, x, mod, n2, w_out_l, l, n_prompt)
        res = _mlp_call(h2, x, mod, w_up_l, w_down_l, fg, l, not more, n_prompt)
        if more:
            x, r = res
            mod = mod_next

    y_p, y_s = res
    return (y_p.reshape(bp, lp, D_MODEL), y_s.reshape(bs, ls, D_MODEL),
            new_p[0], new_p[1], new_s[0], new_s[1])
```

```python
import functools

import numpy as np
import jax
import jax.numpy as jnp
from jax import lax
from jax.experimental import pallas as pl
from jax.experimental.pallas import tpu as pltpu

F32 = jnp.float32
BF16 = jnp.bfloat16

D_MODEL = 2048
DEPTH = 4
H_A = 8
H_B = 4
DK_B = 256
N_PROJ = 8192
N_MOD = 6
D_FF = 4 * D_MODEL
EPS = 1e-6
ROPE_BASE = 10000.0
LOG2_E = 1.4426950408889634
PAST_LEN = 2048

LANES = 128
N_PBLK = N_PROJ // LANES
N_OBLK = D_MODEL // LANES
GROUP = 64
TM = 512
N_SEQ_S = TM // GROUP
MOD_ROWS = 16
HGRN_GROUP = 8

VMEM_LIMIT = 56 * 1024 * 1024

_NT = (((1,), (1,)), ((), ()))
_TN = (((0,), (0,)), ((), ()))

SHIFT1, SCALE1, GATE1, SHIFT2, SCALE2, GATE2 = range(N_MOD)


def _cparams(sem):
    return pltpu.CompilerParams(dimension_semantics=sem, vmem_limit_bytes=VMEM_LIMIT)


def _silu(x):
    hx = 0.5 * x
    return hx * jnp.tanh(hx) + hx


def _rms(x):
    return x * lax.rsqrt(jnp.mean(x * x, axis=-1, keepdims=True) + EPS)


def _mod_specs(which, ngrid):
    if ngrid == 1:
        return [pl.BlockSpec((1, N_SEQ_S, 1, D_MODEL), lambda i: (0, 0, 0, which)),
                pl.BlockSpec((1, 1, 1, D_MODEL), lambda i: (0, N_SEQ_S, 0, which))]
    return [pl.BlockSpec((1, N_SEQ_S, 1, D_MODEL), lambda i, j: (0, 0, 0, which)),
            pl.BlockSpec((1, 1, 1, D_MODEL), lambda i, j: (0, N_SEQ_S, 0, which))]


def _mod_row(is_sample, s_ref, p_ref, gi):
    if isinstance(is_sample, bool):
        return s_ref[0, gi] if is_sample else p_ref[0, 0]
    return jnp.where(is_sample, s_ref[0, gi], p_ref[0, 0])


def _ada_kernel(c_ref, w_ref, b_ref, o_ref):
    c = _silu(c_ref[...]).astype(BF16)
    w = w_ref[0].astype(BF16)
    res = jnp.dot(c, w, preferred_element_type=F32) + b_ref[0]
    for r in range(MOD_ROWS):
        o_ref[0, r] = res[r:r + 1, :]


def _ada_args(c_all, w_ada, b_ada):
    return c_all, w_ada, b_ada.reshape(DEPTH, 1, N_MOD * D_MODEL)


def _ada_call(ada_args, l, tn=1024):
    n = N_MOD * D_MODEL
    return pl.pallas_call(
        _ada_kernel,
        grid=(n // tn,),
        in_specs=[
            pl.BlockSpec((MOD_ROWS, D_MODEL), lambda j: (0, 0)),
            pl.BlockSpec((1, D_MODEL, tn), lambda j: (l, 0, j)),
            pl.BlockSpec((1, 1, tn), lambda j: (l, 0, j)),
        ],
        out_specs=pl.BlockSpec((1, MOD_ROWS, 1, tn), lambda j: (0, 0, 0, j)),
        out_shape=jax.ShapeDtypeStruct((1, MOD_ROWS, 1, n), F32),
        compiler_params=_cparams(("arbitrary",)),
        name="ada_mod",
    )(*ada_args)


def _inproj_tile(x_ref, r_ref, g_ref, scs_ref, scp_ref, shs_ref, shp_ref, w_ref, o_ref, is_sample,
                 tn):
    g = g_ref[0]
    hs = []
    for gi in range(TM // GROUP):
        rows = pl.ds(gi * GROUP, GROUP)
        gain = g * (1.0 + _mod_row(is_sample, scs_ref, scp_ref, gi))
        shift = _mod_row(is_sample, shs_ref, shp_ref, gi)
        hs.append((x_ref[rows, :] * r_ref[rows, :] * gain + shift).astype(BF16))
    h = jnp.concatenate(hs, axis=0)
    acc = jnp.dot(h, w_ref[...], preferred_element_type=F32)
    for c in range(tn // LANES):
        o_ref[c] = acc[:, c * LANES:(c + 1) * LANES].astype(BF16)


def _inproj_first_kernel(xp_ref, xs_ref, g_ref, scs_ref, scp_ref, shs_ref, shp_ref, w_ref, o_ref,
                         x_ref, r_scr, *, tn, n_prompt_tiles):
    is_sample = pl.program_id(0) >= n_prompt_tiles

    @pl.when(pl.program_id(1) == 0)
    def _():
        @pl.when(is_sample)
        def _():
            x_ref[...] = xs_ref[...]

        @pl.when(jnp.logical_not(is_sample))
        def _():
            x_ref[...] = xp_ref[...]
        x = x_ref[...]
        r_scr[...] = lax.rsqrt(jnp.mean(x * x, axis=-1, keepdims=True) + EPS)

    _inproj_tile(x_ref, r_scr, g_ref, scs_ref, scp_ref, shs_ref, shp_ref, w_ref, o_ref, is_sample,
                 tn)


def _inproj_kernel(x_ref, r_ref, g_ref, scs_ref, scp_ref, shs_ref, shp_ref, w_ref, *rest,
                   tn, n_prompt_tiles, n_cast):
    cast_in, o_ref, cast_out = rest[:n_cast], rest[n_cast], rest[n_cast + 1:]
    for src, dst in zip(cast_in, cast_out):
        dst[...] = src[0].astype(BF16)
    is_sample = pl.program_id(1) >= n_prompt_tiles
    _inproj_tile(x_ref, r_ref, g_ref, scs_ref, scp_ref, shs_ref, shp_ref, w_ref, o_ref, is_sample,
                 tn)


def _inproj_first_call(x_prompt, x_sample, mod, norm1_g, w_in_l, l, tn=2048):
    n_prompt = x_prompt.shape[0]
    t = n_prompt + x_sample.shape[0]
    npt = n_prompt // TM
    kern = functools.partial(_inproj_first_kernel, tn=tn, n_prompt_tiles=npt)
    tile = pl.BlockSpec((TM, D_MODEL), lambda i, j: (i, 0))
    return pl.pallas_call(
        kern,
        grid=(t // TM, N_PROJ // tn),
        in_specs=[
            pl.BlockSpec((TM, D_MODEL), lambda i, j: (jnp.minimum(i, npt - 1), 0)),
            pl.BlockSpec((TM, D_MODEL), lambda i, j: (jnp.maximum(i - npt, 0), 0)),
            pl.BlockSpec((1, 1, D_MODEL), lambda i, j: (l, 0, 0)),
            *_mod_specs(SCALE1, 2),
            *_mod_specs(SHIFT1, 2),
            pl.BlockSpec((D_MODEL, tn), lambda i, j: (0, j)),
        ],
        out_specs=[pl.BlockSpec((tn // LANES, TM, LANES), lambda i, j: (j, i, 0)), tile],
        out_shape=[jax.ShapeDtypeStruct((N_PBLK, t, LANES), BF16),
                   jax.ShapeDtypeStruct((t, D_MODEL), F32)],
        scratch_shapes=[pltpu.VMEM((TM, 1), F32)],
        compiler_params=_cparams(("arbitrary", "arbitrary")),
        name="in_proj_first",
    )(x_prompt, x_sample, norm1_g, mod, mod, mod, mod, w_in_l)


def _inproj_call(x, r, mod, norm1_g, w_in_l, casts, l, n_prompt, tn=2048):
    t = x.shape[0]
    npt = n_prompt // TM
    n_slab = (N_PROJ // tn) * npt
    kern = functools.partial(_inproj_kernel, tn=tn, n_prompt_tiles=npt, n_cast=len(casts))
    slab = lambda j, i: j * npt + jnp.minimum(i, npt - 1)
    in_specs = [
        pl.BlockSpec((TM, D_MODEL), lambda j, i: (i, 0)),
        pl.BlockSpec((TM, 1), lambda j, i: (i, 0)),
        pl.BlockSpec((1, 1, D_MODEL), lambda j, i: (l, 0, 0)),
        *_mod_specs(SCALE1, 2),
        *_mod_specs(SHIFT1, 2),
        pl.BlockSpec((D_MODEL, tn), lambda j, i: (0, j)),
    ]
    args = [x, r, norm1_g, mod, mod, mod, mod, w_in_l]
    out_specs = [pl.BlockSpec((tn // LANES, TM, LANES), lambda j, i: (j, i, 0))]
    out_shape = [jax.ShapeDtypeStruct((N_PBLK, t, LANES), BF16)]
    for w, layer in casts:
        _, rows, cols = w.shape
        rb = rows // n_slab
        in_specs.append(pl.BlockSpec((1, rb, cols),
                                     lambda j, i, layer=layer: (layer, slab(j, i), 0)))
        args.append(w)
        out_specs.append(pl.BlockSpec((rb, cols), lambda j, i: (slab(j, i), 0)))
        out_shape.append(jax.ShapeDtypeStruct((rows, cols), BF16))
    return pl.pallas_call(
        kern,
        grid=(N_PROJ // tn, t // TM),
        in_specs=in_specs,
        out_specs=out_specs,
        out_shape=out_shape,
        compiler_params=_cparams(("arbitrary", "arbitrary")),
        name="in_proj",
    )(*args)


def _hgrn_ref_rows(b_scr, lvl, c):
    half = 1 << lvl
    m = 2 * half
    if half >= 8:
        blocks = [jnp.broadcast_to(b_scr[pl.ds(j * m + half - 1, 1), :], (m, LANES))
                  for j in range(c // m)]
    elif half == 4:
        blocks = [jnp.broadcast_to(b_scr[pl.ds(8 * j + 3, 1), :], (8, LANES))
                  for j in range(c // 8)]
    else:
        sub = lax.broadcasted_iota(jnp.int32, (8, LANES), 0)
        blocks = [jnp.where(sub < 4,
                            jnp.broadcast_to(b_scr[pl.ds(8 * j + 1, 1), :], (8, LANES)),
                            jnp.broadcast_to(b_scr[pl.ds(8 * j + 5, 1), :], (8, LANES)))
                  for j in range(c // 8)]
    return blocks[0] if len(blocks) == 1 else jnp.concatenate(blocks, axis=0)


def _pick_rows(row, q, k, lvl, c):
    half = 1 << lvl
    if half < 8:
        return jnp.where((row & half) != 0, q, k)
    return jnp.concatenate([(q if j & 1 else k)[j * half:(j + 1) * half]
                            for j in range(c // half)], axis=0)


def _mixer_kernel(*refs, c, ca, n_chunks, has_state, n_alias, n_cast, with_ada):
    refs = list(refs)
    (proj_ref, cos_ref, sin_ref, lb_ref, ghn_ref, grn_ref, lv_ref, tri_ref, dmat_ref, xi_ref,
     zeta_ref, gc_ref) = refs[:12]
    pos = 12
    if has_state:
        sa_in_ref, sb_in_ref = refs[pos:pos + 2]
        pos += 2
    pos += n_alias
    cast_in = refs[pos:pos + n_cast]
    pos += n_cast
    ada_in = refs[pos:pos + 3 * with_ada]
    pos += 3 * with_ada
    o_ref, sa_out_ref, sb_out_ref = refs[pos:pos + 3]
    pos += 3
    cast_out = refs[pos:pos + n_cast]
    pos += n_cast
    ada_out = refs[pos:pos + with_ada]
    pos += with_ada
    st_scr, sb_scr, b_scr = refs[pos:]

    ci = pl.program_id(1)
    n_lvl = ca.bit_length() - 1

    @pl.when(ci == 0)
    def _():
        if has_state:
            for h in range(H_A):
                st_scr[h] = sa_in_ref[0, 0, h].T
            sb_scr[...] = sb_in_ref[0, 0]
        else:
            st_scr[...] = jnp.zeros_like(st_scr)
            sb_scr[...] = jnp.zeros_like(sb_scr)

    if with_ada:
        _ada_kernel(*ada_in, *ada_out)
    for src, dst in zip(cast_in, cast_out):
        dst[...] = src[0].astype(BF16)

    row = lax.broadcasted_iota(jnp.int32, (ca, LANES), 0)

    n_sub = c // ca

    def hgrn_group(heads):
        streams = [(h, sub) for h in heads for sub in range(n_sub)]
        vals = []
        for h, sub in streams:
            rows = pl.ds(sub * ca, ca)
            lb = lb_ref[h]
            f = lb + (1.0 - lb) * jax.nn.sigmoid(proj_ref[H_A + h, rows, :].astype(F32))
            logf = jnp.log(f) * LOG2_E
            hi = logf.astype(BF16)
            lo = (logf - hi.astype(F32)).astype(BF16)
            b2 = jnp.dot(tri_ref[...], jnp.concatenate([hi, lo], axis=-1),
                         preferred_element_type=F32)
            vals.append(dict(f=f, b2=b2))

        for si, (h, sub) in enumerate(streams):
            d = vals[si]
            f, b2 = d["f"], d["b2"]
            q = _silu(proj_ref[h, pl.ds(sub * ca, ca), :].astype(F32))
            k = 1.0 - f
            b = b2[:, :LANES] + b2[:, LANES:]
            b_sub = b_scr.at[si]
            b_sub[...] = b
            z = jnp.where((row & 1) != 0, q * f, k).astype(BF16)
            p = lax.dot_general(z, z, _NT, preferred_element_type=F32)
            a = jnp.where(lv_ref[...] == 0, p.astype(BF16), jnp.zeros((), BF16))
            for lvl in range(1, n_lvl):
                e = jnp.exp2(-jnp.abs(b - _hgrn_ref_rows(b_sub, lvl, ca)))
                z = (_pick_rows(row, q, k, lvl, ca) * e).astype(BF16)
                p = lax.dot_general(z, z, _NT, preferred_element_type=F32)
                a = jnp.where(lv_ref[...] == lvl, p.astype(BF16), a)
            vb = proj_ref[2 * H_A + h, pl.ds(sub * ca, ca), :]
            b_last = b[ca - 1:ca, :]
            kd = (k * jnp.exp2(b_last - b)).astype(BF16)
            d.update(a=a, upd=lax.dot_general(vb, kd, _TN, preferred_element_type=F32),
                     decay=jnp.exp2(b_last), qi=(q * jnp.exp2(b)).astype(BF16),
                     diag=jnp.sum(q * k, axis=-1, keepdims=True))

        for gi, h in enumerate(heads):
            s_cur = st_scr[h]
            for sub in range(n_sub):
                d = vals[gi * n_sub + sub]
                vb = proj_ref[2 * H_A + h, pl.ds(sub * ca, ca), :]
                o = jnp.dot(d["a"], vb, preferred_element_type=F32)
                d["o"] = o + lax.dot_general(d["qi"], s_cur.astype(BF16), _NT,
                                             preferred_element_type=F32)
                s_cur = s_cur * d["decay"] + d["upd"]
            st_scr[h] = s_cur

        for si, (h, sub) in enumerate(streams):
            rows = pl.ds(sub * ca, ca)
            d = vals[si]
            o = d["o"] + d["diag"] * proj_ref[2 * H_A + h, rows, :].astype(F32)
            ga = proj_ref[3 * H_A + h, rows, :].astype(F32)
            o_ref[h, rows, :] = (_rms(o) * ghn_ref[h] * _silu(ga)).astype(BF16)

    for g0 in range(0, H_A, HGRN_GROUP):
        hgrn_group(list(range(g0, g0 + HGRN_GROUP)))

    cos = cos_ref[...]
    sin = sin_ref[...]

    rvals = []
    for hb in range(H_B):
        def pair(base):
            return (proj_ref[base + 2 * hb].astype(F32), proj_ref[base + 2 * hb + 1].astype(F32))

        q1, q2 = pair(4 * H_A)
        k1, k2 = pair(4 * H_A + 2 * H_B)
        vbase = 4 * H_A + 4 * H_B
        vb = jnp.concatenate([proj_ref[vbase + 2 * hb], proj_ref[vbase + 2 * hb + 1]], axis=-1)
        q = jnp.concatenate([q1 * cos - q2 * sin, q2 * cos + q1 * sin], axis=-1)
        k = jnp.concatenate([k1 * cos - k2 * sin, k2 * cos + k1 * sin], axis=-1) * (DK_B ** -0.5)
        qb = q.astype(BF16)
        s = sb_scr[hb]
        qk = lax.dot_general(qb, k.astype(BF16), _NT, preferred_element_type=F32)
        qs = jnp.dot(qb, s.astype(BF16), preferred_element_type=F32)
        kz = (k * zeta_ref[hb]).astype(BF16)
        sb_scr[hb] = gc_ref[hb] * s + lax.dot_general(kz, vb, _TN, preferred_element_type=F32)
        rvals.append((qk, qs, vb))
    for hb in range(H_B):
        qk, qs, vb = rvals[hb]
        o = jnp.dot((qk * dmat_ref[hb]).astype(BF16), vb, preferred_element_type=F32)
        o = o + qs * xi_ref[hb]
        gbase = 4 * H_A + 6 * H_B
        gate = jnp.concatenate([proj_ref[gbase + 2 * hb].astype(F32),
                                proj_ref[gbase + 2 * hb + 1].astype(F32)], axis=-1)
        out = _rms(o) * grn_ref[hb] * _silu(gate)
        o_ref[H_A + 2 * hb] = out[:, :LANES].astype(BF16)
        o_ref[H_A + 2 * hb + 1] = out[:, LANES:].astype(BF16)

    @pl.when(ci == n_chunks - 1)
    def _():
        for h in range(H_A):
            sa_out_ref[0, 0, h] = st_scr[h].T
        sb_out_ref[0, 0] = sb_scr[...]


def _mixer_tables(c, ca, log_gamma):
    t = np.arange(ca)
    x = t[:, None] ^ t[None, :]
    lv = np.where(t[:, None] > t[None, :], np.floor(np.log2(np.maximum(x, 1))), -1).astype(np.int32)
    tri = (t[:, None] >= t[None, :]).astype(np.float32)
    idx = jnp.arange(c, dtype=F32)
    lg = log_gamma[:, None, None]
    causal = jnp.asarray(np.tril(np.ones((c, c), dtype=bool)))
    dmat = jnp.where(causal[None], jnp.exp((idx[:, None] - idx[None, :])[None] * lg), 0.0)
    xi = jnp.exp((idx + 1.0)[None, :] * log_gamma[:, None])[..., None]
    zeta = jnp.exp((c - 1.0 - idx)[None, :] * log_gamma[:, None])[..., None]
    g_c = jnp.exp(c * log_gamma)[:, None, None]
    return dict(
        lv=jnp.asarray(lv, dtype=BF16), tri=jnp.asarray(tri, dtype=BF16), dmat=dmat,
        xi=jnp.broadcast_to(xi, (H_B, c, DK_B)), zeta=jnp.broadcast_to(zeta, (H_B, c, DK_B)),
        gc=jnp.broadcast_to(g_c, (H_B, 1, DK_B)))


def _mixer_call(proj, cos, sin, lb, ghn, grn, tab, states, prev_states, casts, ada, *, c, ca, batch,
                n_chunks, row0, l):
    has_state = states is not None
    blk0 = row0 // c
    const3 = lambda b, i: (0, 0, 0)
    sa_spec = pl.BlockSpec((1, 1, H_A, LANES, LANES), lambda b, i: (l, b, 0, 0, 0))
    sb_spec = pl.BlockSpec((1, 1, H_B, DK_B, DK_B), lambda b, i: (l, b, 0, 0, 0))
    in_specs = [
        pl.BlockSpec((N_PBLK, c, LANES), lambda b, i: (0, blk0 + b * n_chunks + i, 0)),
        pl.BlockSpec((c, LANES), lambda b, i: (i, 0)),
        pl.BlockSpec((c, LANES), lambda b, i: (i, 0)),
        pl.BlockSpec((H_A, 1, LANES), const3),
        pl.BlockSpec((H_A, 1, LANES), const3),
        pl.BlockSpec((H_B, 1, DK_B), const3),
        pl.BlockSpec((ca, ca), lambda b, i: (0, 0)),
        pl.BlockSpec((ca, ca), lambda b, i: (0, 0)),
        pl.BlockSpec((H_B, c, c), const3),
        pl.BlockSpec((H_B, c, DK_B), const3),
        pl.BlockSpec((H_B, c, DK_B), const3),
        pl.BlockSpec((H_B, 1, DK_B), const3),
    ]
    args = [proj, cos, sin, lb, ghn, grn, tab["lv"], tab["tri"], tab["dmat"], tab["xi"],
            tab["zeta"], tab["gc"]]
    if has_state:
        in_specs += [sa_spec, sb_spec]
        args += list(states)
    aliases = {}
    if prev_states is not None:
        for k, arr in enumerate(prev_states):
            aliases[len(args)] = 1 + k
            in_specs.append(pl.BlockSpec(memory_space=pl.ANY))
            args.append(arr)
    out_specs = [
        pl.BlockSpec((N_OBLK, c, LANES), lambda b, i: (0, b * n_chunks + i, 0)), sa_spec, sb_spec]
    out_shape = [
        jax.ShapeDtypeStruct((N_OBLK, batch * n_chunks * c, LANES), BF16),
        jax.ShapeDtypeStruct((DEPTH, batch, H_A, LANES, LANES), F32),
        jax.ShapeDtypeStruct((DEPTH, batch, H_B, DK_B, DK_B), F32),
    ]
    assert not casts or batch == 1
    for w, layer in casts:
        _, r, cols = w.shape
        rb = r // n_chunks
        in_specs.append(pl.BlockSpec((1, rb, cols), lambda b, i, layer=layer: (layer, i, 0)))
        args.append(w)
        out_specs.append(pl.BlockSpec((rb, cols), lambda b, i: (i, 0)))
        out_shape.append(jax.ShapeDtypeStruct((r, cols), BF16))
    if ada is not None:
        assert batch == 1
        ada_args, layer = ada
        n = N_MOD * D_MODEL
        tn = n // n_chunks
        in_specs += [pl.BlockSpec((MOD_ROWS, D_MODEL), lambda b, i: (0, 0)),
                     pl.BlockSpec((1, D_MODEL, tn), lambda b, i: (layer, 0, i)),
                     pl.BlockSpec((1, 1, tn), lambda b, i: (layer, 0, i))]
        args += list(ada_args)
        out_specs.append(pl.BlockSpec((1, MOD_ROWS, 1, tn), lambda b, i: (0, 0, 0, i)))
        out_shape.append(jax.ShapeDtypeStruct((1, MOD_ROWS, 1, n), F32))
    kern = functools.partial(_mixer_kernel, c=c, ca=ca, n_chunks=n_chunks, has_state=has_state,
                             n_alias=len(aliases), n_cast=len(casts), with_ada=int(ada is not None))
    return pl.pallas_call(
        kern,
        grid=(batch, n_chunks),
        in_specs=in_specs,
        out_specs=out_specs,
        out_shape=out_shape,
        input_output_aliases=aliases,
        scratch_shapes=[
            pltpu.VMEM((H_A, LANES, LANES), F32),
            pltpu.VMEM((H_B, DK_B, DK_B), F32),
            pltpu.VMEM((HGRN_GROUP * (c // ca), ca, LANES), F32),
        ],
        compiler_params=_cparams(("arbitrary", "arbitrary")),
        name="mixer_state" if has_state else "mixer_prompt",
    )(*args)


def _outproj_kernel(op_ref, os_ref, x_ref, w_ref, gts_ref, gtp_ref, g2_ref, scs_ref, scp_ref,
                    shs_ref, shp_ref, xo_ref, h_ref, *, n_prompt_tiles):
    i = pl.program_id(0)
    half = TM // 2

    def run(o_ref, is_sample):
        for hf in range(2):
            o = jnp.concatenate([o_ref[cb, pl.ds(hf * half, half), :] for cb in range(N_OBLK)],
                                axis=-1)
            y = jnp.dot(o, w_ref[...], preferred_element_type=F32)
            for gl in range(half // GROUP):
                gi = hf * (half // GROUP) + gl
                rows = pl.ds(gi * GROUP, GROUP)
                xn = (x_ref[rows, :] + _mod_row(is_sample, gts_ref, gtp_ref, gi)
                      * y[gl * GROUP:(gl + 1) * GROUP, :])
                xo_ref[rows, :] = xn
                gain = g2_ref[0] * (1.0 + _mod_row(is_sample, scs_ref, scp_ref, gi))
                h_ref[rows, :] = (_rms(xn) * gain
                                  + _mod_row(is_sample, shs_ref, shp_ref, gi)).astype(BF16)

    @pl.when(i < n_prompt_tiles)
    def _():
        run(op_ref, False)

    @pl.when(i >= n_prompt_tiles)
    def _():
        run(os_ref, True)


def _outproj_call(o_p, o_s, x, mod, norm2_g, w_out_l, l, n_prompt):
    t = x.shape[0]
    npt = n_prompt // TM
    kern = functools.partial(_outproj_kernel, n_prompt_tiles=npt)
    last_p = npt - 1
    return pl.pallas_call(
        kern,
        grid=(t // TM,),
        in_specs=[
            pl.BlockSpec((N_OBLK, TM, LANES), lambda i: (0, jnp.minimum(i, last_p), 0)),
            pl.BlockSpec((N_OBLK, TM, LANES), lambda i: (0, jnp.maximum(i - npt, 0), 0)),
            pl.BlockSpec((TM, D_MODEL), lambda i: (i, 0)),
            pl.BlockSpec((D_MODEL, D_MODEL), lambda i: (0, 0)),
            *_mod_specs(GATE1, 1),
            pl.BlockSpec((1, 1, D_MODEL), lambda i: (l, 0, 0)),
            *_mod_specs(SCALE2, 1),
            *_mod_specs(SHIFT2, 1),
        ],
        out_specs=[
            pl.BlockSpec((TM, D_MODEL), lambda i: (i, 0)),
            pl.BlockSpec((TM, D_MODEL), lambda i: (i, 0)),
        ],
        out_shape=[
            jax.ShapeDtypeStruct((t, D_MODEL), F32),
            jax.ShapeDtypeStruct((t, D_MODEL), BF16),
        ],
        compiler_params=_cparams(("arbitrary",)),
        name="out_proj",
    )(o_p, o_s, x, w_out_l, mod, mod, norm2_g, mod, mod, mod, mod)


def _mlp_kernel(h_ref, x_ref, wu_ref, wd_ref, gts_ref, gtp_ref, fg_ref, *rest, nf, final,
                n_prompt_tiles):
    if final:
        yp_ref, ys_ref, acc_ref = rest
    else:
        xo_ref, r_ref, acc_ref = rest
    i = pl.program_id(0)
    f = pl.program_id(1)

    @pl.when(jnp.logical_and(i == 0, f == 0))
    def _():
        acc_ref[...] = jnp.zeros_like(acc_ref)

    u = jnp.dot(h_ref[...], wu_ref[...], preferred_element_type=F32)
    u = jnp.square(jnp.maximum(u, 0.0)).astype(BF16)
    acc_ref[...] += jnp.dot(u, wd_ref[...], preferred_element_type=F32)

    def finish(dst_ref, is_sample):
        for gi in range(TM // GROUP):
            rows = pl.ds(gi * GROUP, GROUP)
            xn = x_ref[rows, :] + _mod_row(is_sample, gts_ref, gtp_ref, gi) * acc_ref[rows, :]
            acc_ref[rows, :] = jnp.zeros((GROUP, D_MODEL), F32)
            r = lax.rsqrt(jnp.mean(xn * xn, axis=-1, keepdims=True) + EPS)
            if final:
                xn = xn * r * fg_ref[...]
            else:
                r_ref[rows, :] = r
            dst_ref[rows, :] = xn

    @pl.when(f == nf - 1)
    def _():
        if final:
            @pl.when(i < n_prompt_tiles)
            def _():
                finish(yp_ref, False)

            @pl.when(i >= n_prompt_tiles)
            def _():
                finish(ys_ref, True)
        else:
            finish(xo_ref, i >= n_prompt_tiles)


def _mlp_call(h, x, mod, w_up_l, w_down_l, final_g, l, final, n_prompt, tf=1024):
    t = x.shape[0]
    nf = D_FF // tf
    npt = n_prompt // TM
    kern = functools.partial(_mlp_kernel, nf=nf, final=final, n_prompt_tiles=npt)
    if final:
        last_p = npt - 1
        out_specs = [pl.BlockSpec((TM, D_MODEL), lambda i, f: (jnp.minimum(i, last_p), 0)),
                     pl.BlockSpec((TM, D_MODEL), lambda i, f: (jnp.maximum(i - npt, 0), 0))]
        out_shape = [jax.ShapeDtypeStruct((n_prompt, D_MODEL), F32),
                     jax.ShapeDtypeStruct((t - n_prompt, D_MODEL), F32)]
    else:
        out_specs = [pl.BlockSpec((TM, D_MODEL), lambda i, f: (i, 0)),
                     pl.BlockSpec((TM, 1), lambda i, f: (i, 0))]
        out_shape = [jax.ShapeDtypeStruct((t, D_MODEL), F32), jax.ShapeDtypeStruct((t, 1), F32)]
    return pl.pallas_call(
        kern,
        grid=(t // TM, nf),
        in_specs=[
            pl.BlockSpec((TM, D_MODEL), lambda i, f: (i, 0)),
            pl.BlockSpec((TM, D_MODEL), lambda i, f: (i, 0)),
            pl.BlockSpec((D_MODEL, tf), lambda i, f: (0, f)),
            pl.BlockSpec((tf, D_MODEL), lambda i, f: (f, 0)),
            *_mod_specs(GATE2, 2),
            pl.BlockSpec((1, D_MODEL), lambda i, f: (0, 0)),
        ],
        out_specs=out_specs,
        out_shape=out_shape,
        scratch_shapes=[pltpu.VMEM((TM, D_MODEL), F32)],
        compiler_params=_cparams(("arbitrary", "arbitrary")),
        name="mlp_final" if final else "mlp",
    )(h, x, w_up_l, w_down_l, mod, mod, final_g)


def _rope_tables(pos):
    half = DK_B // 2
    inv_freq = 1.0 / (ROPE_BASE ** jnp.linspace(0.0, 1.0, half, dtype=F32))
    ang = pos[:, None] * inv_freq[None, :]
    return jnp.cos(ang), jnp.sin(ang)


def kernel(x_prompt, x_sample, state_hgrn, state_ret, c_prompt, c_sample, lb_logits, w_ada, b_ada,
           norm1_g, norm2_g, w_in, hgrn_norm_g, ret_norm_g, w_out, w_up, w_down, final_g):
    bp, lp, _ = x_prompt.shape
    bs, ls, _ = x_sample.shape
    assert bp == 1 and ls == GROUP and bs == N_SEQ_S and lp % TM == 0
    n_prompt = bp * lp
    n_sample = bs * ls
    c_prompt_chunk = 256
    ca_prompt = 128

    c_all = jnp.concatenate(
        [c_sample, c_prompt, jnp.zeros((MOD_ROWS - bs - bp, D_MODEL), F32)], 0)
    ada_args = _ada_args(c_all, w_ada, b_ada)
    mod = _ada_call(ada_args, 0)

    p = jax.nn.softmax(lb_logits.astype(F32), axis=0)
    cs = jnp.cumsum(p, axis=0)
    lb_all = (cs - cs[0:1]).reshape(DEPTH, H_A, 1, LANES)
    log_gamma = jnp.log1p(-jnp.exp2(-5.0 - jnp.arange(H_B, dtype=F32)))
    ghn = hgrn_norm_g.reshape(DEPTH, H_A, 1, LANES)
    grn = ret_norm_g.reshape(DEPTH, H_B, 1, DK_B)
    n1 = norm1_g.reshape(DEPTH, 1, D_MODEL)
    n2 = norm2_g.reshape(DEPTH, 1, D_MODEL)
    fg = final_g.reshape(1, D_MODEL)

    cos_p, sin_p = _rope_tables(jnp.arange(lp, dtype=F32))
    cos_s, sin_s = _rope_tables(PAST_LEN + jnp.arange(ls, dtype=F32))
    tab_p = _mixer_tables(c_prompt_chunk, ca_prompt, log_gamma)
    tab_s = _mixer_tables(ls, ls, log_gamma)

    w_in_l = w_in[0].astype(BF16)
    new_p = (jnp.zeros((DEPTH, bp, H_A, LANES, LANES), F32), jnp.zeros((DEPTH, bp, H_B, DK_B, DK_B), F32))
    new_s = (jnp.zeros((DEPTH, bs, H_A, LANES, LANES), F32), jnp.zeros((DEPTH, bs, H_B, DK_B, DK_B), F32))
    for l in range(DEPTH):
        if l == 0:
            proj, x = _inproj_first_call(x_prompt.reshape(n_prompt, D_MODEL),
                                         x_sample.reshape(n_sample, D_MODEL), mod, n1, w_in_l, l)
        else:
            proj, w_up_l, w_down_l = _inproj_call(x, r, mod, n1, w_in_l,
                                                  [(w_up, l), (w_down, l)], l, n_prompt)
        more = l + 1 < DEPTH
        casts = [(w_out, l)] + ([(w_up, l), (w_down, l)] if l == 0 else [])
        casts += [(w_in, l + 1)] if more else []
        res = _mixer_call(proj, cos_p, sin_p, lb_all[l], ghn[l], grn[l], tab_p, None, new_p, casts,
                          (ada_args, l + 1) if more else None,
                          c=c_prompt_chunk, ca=ca_prompt, batch=bp,
                          n_chunks=lp // c_prompt_chunk, row0=0, l=l)
        o_p, w_out_l = res[0], res[3]
        new_p = (res[1], res[2])
        if l == 0:
            w_up_l, w_down_l = res[4], res[5]
        if more:
            w_in_l, mod_next = res[3 + len(casts) - 1], res[3 + len(casts)]
        res = _mixer_call(proj, cos_s, sin_s, lb_all[l], ghn[l], grn[l], tab_s,
                          (state_hgrn, state_ret), new_s, [], None,
                          c=ls, ca=ls, batch=bs, n_chunks=1, row0=n_prompt, l=l)
        o_s = res[0]
        new_s = (res[1], res[2])
        x, h2 = _outproj_call(o_p, o_s, x, mod, n2, w_out_l, l, n_prompt)
        res = _mlp_call(h2, x, mod, w_up_l, w_down_l, fg, l, not more, n_prompt)
        if more:
            x, r = res
            mod = mod_next

    y_p, y_s = res
    return (y_p.reshape(bp, lp, D_MODEL), y_s.reshape(bs, ls, D_MODEL),
            new_p[0], new_p[1], new_s[0], new_s[1])
```
